```python
import math
import jax, jax.numpy as jnp
from jax import lax
import numpy as np

D_MODEL = 2048
BATCH = 2
SEQ = 16384
DEPTH = 2

GRID_W = 64
CTX_LEN = 256
CHUNK = 128
A_GROUPS = 8
A_DIM = D_MODEL // 2 // A_GROUPS
A_WIDTH = A_GROUPS * A_DIM
B_HEADS = 8
B_DIM = D_MODEL // 2 // B_HEADS
B_WIDTH = B_HEADS * B_DIM
NA_ROWS = 8
NA_COLS = 16
C_WIDTH = D_MODEL
CONV_W = 3
N_EXPERTS = 16
CAP_FACTOR = 2
EXPERT_FF = D_MODEL // 2
N_EVEN = (DEPTH + 1) // 2
N_ODD = DEPTH // 2
EPS = 1e-6

kernel_name = "hybrid_gmlp_natten_shortconv_ecmoe_dit"


def rmsnorm(x, g):
    xf = x.astype(jnp.float32)
    y = xf * lax.rsqrt(jnp.mean(xf * xf, axis=-1, keepdims=True) + EPS)
    return (y * g.astype(jnp.float32)).astype(x.dtype)


def modulate(h, shift, scale):
    return h * (1 + scale) + shift


def softmax32(s):
    return jax.nn.softmax(s.astype(jnp.float32), axis=-1)


def chunk_gmlp(u, v, w_s, b_s, g_v):
    B, N, _ = u.shape
    u = u.reshape(B, N, A_GROUPS, A_DIM)
    vn = rmsnorm(v.reshape(B, N, A_GROUPS, A_DIM), g_v)
    vn = vn.reshape(B, N // CHUNK, CHUNK, A_GROUPS, A_DIM)
    s = jnp.einsum('gpq,bnqgc->bnpgc', w_s, vn) + b_s.T[:, :, None]
    return (u * s.reshape(B, N, A_GROUPS, A_DIM)).reshape(B, N, A_WIDTH)


def context_attention(q, k, v):
    B, L, H, d = q.shape
    s = jnp.einsum('blhd,bmhd->bhlm', q, k).astype(jnp.float32) * (d ** -0.5)
    p = softmax32(s).astype(v.dtype)
    return jnp.einsum('bhlm,bmhd->blhd', p, v).reshape(B, L, H * d)


def neighborhood_attention(q, k, v, kc, vc, rpb):
    B, S, H, d = q.shape
    rows = S // GRID_W
    kr = min(NA_ROWS, rows)
    scale = d ** -0.5
    qg = q.reshape(B, rows, GRID_W, H, d)
    kg = k.reshape(B, rows, GRID_W, H, d)
    vg = v.reshape(B, rows, GRID_W, H, d)
    cols = jnp.arange(GRID_W)
    cstart = jnp.clip(cols - NA_COLS // 2, 0, GRID_W - NA_COLS)
    cidx = cstart[:, None] + jnp.arange(NA_COLS)[None, :]
    col_off = cidx - cols[:, None] + (NA_COLS - 1)
    rpb_col = rpb[:, :, col_off]

    def row_step(r):
        rs = jnp.clip(r - kr // 2, 0, rows - kr)
        qr = lax.dynamic_index_in_dim(qg, r, axis=1, keepdims=False)
        kb = lax.dynamic_slice_in_dim(kg, rs, kr, axis=1)
        vb = lax.dynamic_slice_in_dim(vg, rs, kr, axis=1)
        kw = kb[:, :, cidx]
        vw = vb[:, :, cidx]
        row_off = rs + jnp.arange(kr) - r + (NA_ROWS - 1)
        bias = jnp.transpose(rpb_col[:, row_off], (0, 2, 1, 3))
        s_loc = jnp.einsum('bjhd,brjchd->bhjrc', qr, kw).astype(jnp.float32) * scale + bias[None]
        s_ctx = jnp.einsum('bjhd,blhd->bhjl', qr, kc).astype(jnp.float32) * scale
        s_all = jnp.concatenate([s_loc.reshape(B, H, GRID_W, kr * NA_COLS), s_ctx], axis=-1)
        p = softmax32(s_all).astype(v.dtype)
        p_loc = p[..., :kr * NA_COLS].reshape(B, H, GRID_W, kr, NA_COLS)
        p_ctx = p[..., kr * NA_COLS:]
        return (jnp.einsum('bhjrc,brjchd->bjhd', p_loc, vw)
                + jnp.einsum('bhjl,blhd->bjhd', p_ctx, vc))

    out = lax.map(row_step, jnp.arange(rows))
    return jnp.moveaxis(out, 0, 1).reshape(B, S, H * d)


def split_ab(p, gq, gk):
    B, N, _ = p.shape
    u = jax.nn.gelu(p[..., :A_WIDTH])
    v = jax.nn.gelu(p[..., A_WIDTH:2 * A_WIDTH])
    o = 2 * A_WIDTH
    q = rmsnorm(p[..., o:o + B_WIDTH].reshape(B, N, B_HEADS, B_DIM), gq)
    k = rmsnorm(p[..., o + B_WIDTH:o + 2 * B_WIDTH].reshape(B, N, B_HEADS, B_DIM), gk)
    vv = p[..., o + 2 * B_WIDTH:o + 3 * B_WIDTH].reshape(B, N, B_HEADS, B_DIM)
    return u, v, q, k, vv


def mixer_ab(hl, hc, w_in, w_out, w_s, b_s, g_v, gq, gk, rpb, with_ctx_out):
    ul, vl, ql, kl, vvl = split_ab(hl @ w_in, gq, gk)
    uc, vc, qc, kc, vvc = split_ab(hc @ w_in, gq, gk)
    a_l = chunk_gmlp(ul, vl, w_s, b_s, g_v)
    b_l = neighborhood_attention(ql, kl, vvl, kc, vvc, rpb)
    out_l = jnp.concatenate([a_l, b_l], axis=-1) @ w_out
    out_c = None
    if with_ctx_out:
        a_c = chunk_gmlp(uc, vc, w_s, b_s, g_v)
        b_c = context_attention(qc, kc, vvc)
        out_c = jnp.concatenate([a_c, b_c], axis=-1) @ w_out
    return out_l, out_c


def short_conv(z, w):
    N = z.shape[1]
    pad = CONV_W // 2
    zp = jnp.pad(z, ((0, 0), (pad, pad), (0, 0)))
    y = w[0] * zp[:, 0:N]
    for t in range(1, CONV_W):
        y = y + w[t] * zp[:, t:t + N]
    return y


def mixer_c(h, w_in, conv_w, w_out):
    p = h @ w_in
    bg = p[..., :C_WIDTH]
    cg = p[..., C_WIDTH:2 * C_WIDTH]
    z = p[..., 2 * C_WIDTH:]
    return (bg * short_conv(cg * z, conv_w)) @ w_out


def ec_moe(h, w_r, w_gate, w_up, w_down):
    B, N, D = h.shape
    cap = CAP_FACTOR * N // N_EXPERTS
    aff = softmax32(jnp.einsum('bnd,de->bne', h, w_r))
    g, idx = lax.top_k(jnp.transpose(aff, (0, 2, 1)), cap)
    xs = jax.vmap(lambda hb, ib: hb[ib])(h, idx)
    hid = jax.nn.silu(jnp.einsum('becd,edf->becf', xs, w_gate)) * jnp.einsum('becd,edf->becf', xs, w_up)
    y = jnp.einsum('becf,efd->becd', hid, w_down) * g[..., None].astype(h.dtype)
    return jax.vmap(lambda ib, yb: jnp.zeros((N, D), yb.dtype).at[ib.reshape(-1)].add(yb.reshape(-1, D)))(idx, y)


def setup_inputs(seed: int = 0) -> dict:
    key = jax.random.key(seed)
    ks = jax.random.split(key, 24)
    D = D_MODEL

    def nrm(k, shape, s):
        return jax.random.normal(k, shape, jnp.float32) * s

    return {
        "x": nrm(ks[0], (BATCH, SEQ, D), 1.0),
        "c": nrm(ks[1], (BATCH, D), 1.0),
        "ctx": nrm(ks[2], (BATCH, CTX_LEN, D), 1.0),
        "c_ctx": nrm(ks[3], (D,), 1.0),
        "ada_w": nrm(ks[4], (DEPTH, D, 6 * D), D ** -0.5),
        "ada_b": nrm(ks[5], (DEPTH, 6 * D), 0.02),
        "norm1_g": 1.0 + nrm(ks[6], (DEPTH, D), 0.02),
        "norm2_g": 1.0 + nrm(ks[7], (DEPTH, D), 0.02),
        "ab_w_in": nrm(ks[8], (N_EVEN, D, 2 * A_WIDTH + 3 * B_WIDTH), D ** -0.5),
        "ab_w_out": nrm(ks[9], (N_EVEN, A_WIDTH + B_WIDTH, D), (A_WIDTH + B_WIDTH) ** -0.5),
        "a_ws": nrm(ks[10], (N_EVEN, A_GROUPS, CHUNK, CHUNK), CHUNK ** -0.5),
        "a_bs": 1.0 + nrm(ks[11], (N_EVEN, A_GROUPS, CHUNK), 0.02),
        "a_vnorm_g": 1.0 + nrm(ks[12], (N_EVEN, A_GROUPS, A_DIM), 0.02),
        "b_qnorm_g": 1.0 + nrm(ks[13], (N_EVEN, B_DIM), 0.02),
        "b_knorm_g": 1.0 + nrm(ks[14], (N_EVEN, B_DIM), 0.02),
        "b_rpb": nrm(ks[15], (N_EVEN, B_HEADS, 2 * NA_ROWS - 1, 2 * NA_COLS - 1), 0.1),
        "c_w_in": nrm(ks[16], (N_ODD, D, 3 * C_WIDTH), D ** -0.5),
        "c_conv_w": nrm(ks[17], (N_ODD, CONV_W, C_WIDTH), CONV_W ** -0.5),
        "c_w_out": nrm(ks[18], (N_ODD, C_WIDTH, D), C_WIDTH ** -0.5),
        "router_w": nrm(ks[19], (DEPTH, D, N_EXPERTS), D ** -0.5),
        "moe_w_gate": nrm(ks[20], (DEPTH, N_EXPERTS, D, EXPERT_FF), D ** -0.5),
        "moe_w_up": nrm(ks[21], (DEPTH, N_EXPERTS, D, EXPERT_FF), D ** -0.5),
        "moe_w_down": nrm(ks[22], (DEPTH, N_EXPERTS, EXPERT_FF, D), EXPERT_FF ** -0.5),
    }


def reference(x, c, ctx, c_ctx, ada_w, ada_b, norm1_g, norm2_g, ab_w_in, ab_w_out, a_ws, a_bs,
              a_vnorm_g, b_qnorm_g, b_knorm_g, b_rpb, c_w_in, c_conv_w, c_w_out, router_w,
              moe_w_gate, moe_w_up, moe_w_down):
    xl = x
    xc = ctx
    for i in range(DEPTH):
        upd_ctx = i < DEPTH - 1
        mod_l = (jax.nn.silu(c) @ ada_w[i] + ada_b[i])[:, None, :]
        mod_c = jax.nn.silu(c_ctx) @ ada_w[i] + ada_b[i]
        sh1, sc1, g1, sh2, sc2, g2 = jnp.split(mod_l, 6, axis=-1)
        csh1, csc1, cg1, csh2, csc2, cg2 = jnp.split(mod_c, 6, axis=-1)

        hl = modulate(rmsnorm(xl, norm1_g[i]), sh1, sc1)
        if i % 2 == 0:
            j = i // 2
            hc = modulate(rmsnorm(xc, norm1_g[i]), csh1, csc1)
            ml, mc = mixer_ab(hl, hc, ab_w_in[j], ab_w_out[j], a_ws[j], a_bs[j], a_vnorm_g[j],
                              b_qnorm_g[j], b_knorm_g[j], b_rpb[j], upd_ctx)
        else:
            j = i // 2
            ml = mixer_c(hl, c_w_in[j], c_conv_w[j], c_w_out[j])
            mc = None
            if upd_ctx:
                hc = modulate(rmsnorm(xc, norm1_g[i]), csh1, csc1)
                mc = mixer_c(hc, c_w_in[j], c_conv_w[j], c_w_out[j])
        xl = xl + g1 * ml
        if upd_ctx:
            xc = xc + cg1 * mc

        xl = xl + g2 * ec_moe(modulate(rmsnorm(xl, norm2_g[i]), sh2, sc2),
                              router_w[i], moe_w_gate[i], moe_w_up[i], moe_w_down[i])
        if upd_ctx:
            xc = xc + cg2 * ec_moe(modulate(rmsnorm(xc, norm2_g[i]), csh2, csc2),
                                   router_w[i], moe_w_gate[i], moe_w_up[i], moe_w_down[i])
    return xl
```

```python
import functools
import math

import jax
import jax.numpy as jnp
from jax import lax
from jax.experimental import pallas as pl
from jax.experimental.pallas import tpu as pltpu

GRID_W = 64
CHUNK = 128
A_GROUPS = 8
A_DIM = 128
A_WIDTH = A_GROUPS * A_DIM
B_HEADS = 8
B_DIM = 128
B_WIDTH = B_HEADS * B_DIM
NA_ROWS = 8
NA_COLS = 16
CONV_W = 3
N_EXPERTS = 16
CAP_FACTOR = 2
EPS = 1e-6

LANES = 128
VMEM_LIMIT = 56 * 1024 * 1024

F32 = jnp.float32
BF16 = jnp.bfloat16
HIGHEST = lax.Precision.HIGHEST
MASK_VALUE = -1e30


SLAB_ROWS = 16
PITCH = 20


def _params(sem, **kw):
    return pltpu.CompilerParams(dimension_semantics=sem, vmem_limit_bytes=VMEM_LIMIT, **kw)


def _slab_cols(ref, lead, c, tm):
    return (*lead, pl.ds(c, tm, stride=PITCH), slice(None))


def _slab_zero_pad(ref, lead, tm):
    for c in range(SLAB_ROWS, PITCH):
        ref[_slab_cols(ref, lead, c, tm)] = jnp.zeros((tm, LANES), ref.dtype)


def _slab_load_rows(ref, lead, tm, dst_ref):
    for c in range(SLAB_ROWS):
        dst_ref[:, c * LANES:(c + 1) * LANES] = ref[_slab_cols(ref, lead, c, tm)]


def _silu(x):
    return x * (1.0 / (1.0 + jnp.exp(-x)))


def _gelu_tanh(x):
    return 0.5 * x * (1.0 + jnp.tanh(math.sqrt(2.0 / math.pi) * (x + 0.044715 * (x * x * x))))


def _norm_mod(x, gamma, shift, scale):
    ms = jnp.mean(x * x, axis=-1, keepdims=True)
    return (x * lax.rsqrt(ms + EPS) * gamma) * (1.0 + scale) + shift


def _ada_kernel(c_ref, w_ref, b_ref, o_ref):
    s = _silu(c_ref[...])
    o_ref[0] = jnp.dot(s, w_ref[0], precision=HIGHEST, preferred_element_type=F32) + b_ref[0]


def _ada(cond, ada_w, ada_b):
    depth, d, n6 = ada_w.shape
    tn = 1024
    return pl.pallas_call(
        _ada_kernel,
        grid=(depth, n6 // tn),
        in_specs=[
            pl.BlockSpec((8, d), lambda l, j: (0, 0)),
            pl.BlockSpec((1, d, tn), lambda l, j: (l, 0, j)),
            pl.BlockSpec((1, 1, tn), lambda l, j: (l, 0, j)),
        ],
        out_specs=pl.BlockSpec((1, 8, tn), lambda l, j: (l, 0, j)),
        out_shape=jax.ShapeDtypeStruct((depth, 8, n6), F32),
        compiler_params=_params(("arbitrary", "arbitrary")),
        name="ada_mod",
    )(cond, ada_w, ada_b.reshape(depth, 1, n6))


def _ab_in_kernel(x_ref, gam_ref, sh_ref, sc_ref, w_ref, gain_ref, o_ref, h_ref):
    j = pl.program_id(2)

    @pl.when(j == 0)
    def _():
        h_ref[...] = _norm_mod(x_ref[0], gam_ref[...], sh_ref[0], sc_ref[0]).astype(BF16)

    acc = jnp.dot(h_ref[...], w_ref[...], preferred_element_type=F32)
    tn = acc.shape[1]

    def store_group_norm(a):
        for g in range(tn // LANES):
            sl = slice(g * LANES, (g + 1) * LANES)
            ag = a[:, sl]
            ms = jnp.mean(ag * ag, axis=-1, keepdims=True)
            o_ref[0, :, sl] = (ag * lax.rsqrt(ms + EPS) * gain_ref[:, sl]).astype(o_ref.dtype)

    @pl.when(j == 0)
    def _():
        o_ref[0] = _gelu_tanh(acc).astype(o_ref.dtype)

    @pl.when(j == 1)
    def _():
        store_group_norm(_gelu_tanh(acc))

    @pl.when((j == 2) | (j == 3))
    def _():
        store_group_norm(acc)

    @pl.when(j == 4)
    def _():
        o_ref[0] = acc.astype(o_ref.dtype)


def _ab_in(x, gamma, shift, scale, w, gain, tm):
    b, n, d = x.shape
    f = w.shape[1]
    tn = A_WIDTH
    return pl.pallas_call(
        _ab_in_kernel,
        grid=(b, n // tm, f // tn),
        in_specs=[
            pl.BlockSpec((1, tm, d), lambda bi, i, j: (bi, i, 0)),
            pl.BlockSpec((1, d), lambda bi, i, j: (0, 0)),
            pl.BlockSpec((1, 1, d), lambda bi, i, j: (bi, 0, 0)),
            pl.BlockSpec((1, 1, d), lambda bi, i, j: (bi, 0, 0)),
            pl.BlockSpec((d, tn), lambda bi, i, j: (0, j)),
            pl.BlockSpec((1, tn), lambda bi, i, j: (0, j)),
        ],
        out_specs=pl.BlockSpec((1, tm, tn), lambda bi, i, j: (bi, i, j)),
        out_shape=jax.ShapeDtypeStruct((b, n, f), BF16),
        scratch_shapes=[pltpu.VMEM((tm, d), BF16)],
        compiler_params=_params(("arbitrary", "arbitrary", "arbitrary")),
        name="ab_in_proj",
    )(x, gamma, shift, scale, w, gain)


def _gmlp_kernel(u_ref, v_ref, ws_ref, bs_ref, o_ref):
    tm = u_ref.shape[1]
    for ch in range(tm // CHUNK):
        rows = slice(ch * CHUNK, (ch + 1) * CHUNK)
        for g in range(A_GROUPS):
            cols = slice(g * A_DIM, (g + 1) * A_DIM)
            s = jnp.dot(ws_ref[g], v_ref[0, rows, cols], preferred_element_type=F32) + bs_ref[g]
            o_ref[0, rows, cols] = (u_ref[0, rows, cols].astype(F32) * s).astype(o_ref.dtype)


def _gmlp(p, ws, bs_b, tm):
    b, n, _ = p.shape
    return pl.pallas_call(
        _gmlp_kernel,
        grid=(b, n // tm),
        in_specs=[
            pl.BlockSpec((1, tm, A_WIDTH), lambda bi, i: (bi, i, 0)),
            pl.BlockSpec((1, tm, A_WIDTH), lambda bi, i: (bi, i, 1)),
            pl.BlockSpec((A_GROUPS, CHUNK, CHUNK), lambda bi, i: (0, 0, 0)),
            pl.BlockSpec((A_GROUPS, CHUNK, A_DIM), lambda bi, i: (0, 0, 0)),
        ],
        out_specs=pl.BlockSpec((1, tm, A_WIDTH), lambda bi, i: (bi, i, 0)),
        out_shape=jax.ShapeDtypeStruct((b, n, A_WIDTH), BF16),
        compiler_params=_params(("arbitrary", "arbitrary")),
        name="gmlp",
    )(p, p, ws, bs_b)


ROWS_PER_STEP = NA_ROWS
WIN = NA_ROWS * GRID_W


def _na_bias_table(rpb):
    cols = jnp.arange(GRID_W)
    cstart = jnp.clip(cols - NA_COLS // 2, 0, GRID_W - NA_COLS)
    kc = jnp.arange(GRID_W)
    valid = (kc[None, :] >= cstart[:, None]) & (kc[None, :] < cstart[:, None] + NA_COLS)
    col_off = jnp.clip(kc[None, :] - cols[:, None] + (NA_COLS - 1), 0, 2 * NA_COLS - 2)
    row_off = jnp.arange(NA_ROWS)[None, :] - jnp.arange(NA_ROWS)[:, None] + (NA_ROWS - 1)
    t = rpb[:, row_off][:, :, :, col_off]
    t = jnp.where(valid[None, None, None], t, MASK_VALUE)
    t = jnp.transpose(t, (1, 0, 3, 2, 4))
    return t.reshape(NA_ROWS, rpb.shape[0], GRID_W, WIN).astype(F32)


def _na_kernel(q_ref, kp_ref, kc_ref, kn_ref, vp_ref, vc_ref, vn_ref, kx_ref, vx_ref, bias_ref, o_ref,
               kcat, vcat, *, rows):
    blk = pl.program_id(1)
    tq = ROWS_PER_STEP * GRID_W
    kcat[0:tq] = kp_ref[0]
    kcat[tq:2 * tq] = kc_ref[0]
    kcat[2 * tq:3 * tq] = kn_ref[0]
    vcat[0:tq] = vp_ref[0]
    vcat[tq:2 * tq] = vc_ref[0]
    vcat[2 * tq:3 * tq] = vn_ref[0]
    dn = (((1,), (1,)), ((), ()))

    def row_body(i, carry):
        r = blk * ROWS_PER_STEP + i
        rs = jnp.clip(r - NA_ROWS // 2, 0, rows - NA_ROWS)
        off = pl.multiple_of((rs - (blk - 1) * ROWS_PER_STEP) * GRID_W, GRID_W)
        var = r - rs
        qoff = pl.multiple_of(i * GRID_W, GRID_W)
        for h in range(B_HEADS):
            cols = slice(h * B_DIM, (h + 1) * B_DIM)
            q = q_ref[0, pl.ds(qoff, GRID_W), cols]
            kw = kcat[pl.ds(off, WIN), cols]
            vw = vcat[pl.ds(off, WIN), cols]
            s_loc = lax.dot_general(q, kw, dn, preferred_element_type=F32) + bias_ref[var, h]
            s_ctx = lax.dot_general(q, kx_ref[0, :, cols], dn, preferred_element_type=F32)
            m = jnp.maximum(jnp.max(s_loc, axis=-1, keepdims=True), jnp.max(s_ctx, axis=-1, keepdims=True))
            p_loc = jnp.exp(s_loc - m)
            p_ctx = jnp.exp(s_ctx - m)
            l = jnp.sum(p_loc, axis=-1, keepdims=True) + jnp.sum(p_ctx, axis=-1, keepdims=True)
            o = (jnp.dot(p_loc.astype(BF16), vw, preferred_element_type=F32)
                 + jnp.dot(p_ctx.astype(BF16), vx_ref[0, :, cols], preferred_element_type=F32))
            o_ref[0, pl.ds(qoff, GRID_W), cols] = (o / l).astype(o_ref.dtype)
        return carry

    lax.fori_loop(0, ROWS_PER_STEP, row_body, 0)


def _neighborhood_attention(p, pc, bias):
    b, s, _ = p.shape
    lc = pc.shape[1]
    rows = s // GRID_W
    tq = ROWS_PER_STEP * GRID_W
    nblk = s // tq
    qcol, kcol, vcol = 2, 3, 4
    spec = lambda col, fn: pl.BlockSpec((1, tq, B_WIDTH), lambda bi, i: (bi, fn(i), col))
    prev = lambda i: jnp.maximum(i - 1, 0)
    cur = lambda i: i
    nxt = lambda i: jnp.minimum(i + 1, nblk - 1)
    return pl.pallas_call(
        functools.partial(_na_kernel, rows=rows),
        grid=(b, nblk),
        in_specs=[
            spec(qcol, cur),
            spec(kcol, prev), spec(kcol, cur), spec(kcol, nxt),
            spec(vcol, prev), spec(vcol, cur), spec(vcol, nxt),
            pl.BlockSpec((1, lc, B_WIDTH), lambda bi, i: (bi, 0, kcol)),
            pl.BlockSpec((1, lc, B_WIDTH), lambda bi, i: (bi, 0, vcol)),
            pl.BlockSpec(bias.shape, lambda bi, i: (0, 0, 0, 0)),
        ],
        out_specs=pl.BlockSpec((1, tq, B_WIDTH), lambda bi, i: (bi, i, 0)),
        out_shape=jax.ShapeDtypeStruct((b, s, B_WIDTH), BF16),
        scratch_shapes=[pltpu.VMEM((3 * tq, B_WIDTH), BF16), pltpu.VMEM((3 * tq, B_WIDTH), BF16)],
        compiler_params=_params(("arbitrary", "arbitrary")),
        name="neighborhood_attention",
    )(p, p, p, p, p, p, p, pc, pc, bias)


def _store_slab_tile(o_ref, res, j, tm):
    nc = res.shape[1] // LANES

    @pl.when(j == 0)
    def _():
        _slab_zero_pad(o_ref, (0,), tm)

    for cc in range(nc):
        o_ref[_slab_cols(o_ref, (0,), j * nc + cc, tm)] = res[:, cc * LANES:(cc + 1) * LANES]


def _ab_out_kernel(a_ref, b_ref, wa_ref, wb_ref, x_ref, g_ref, o_ref):
    acc = jnp.dot(a_ref[0], wa_ref[...], preferred_element_type=F32)
    acc += jnp.dot(b_ref[0], wb_ref[...], preferred_element_type=F32)
    _store_slab_tile(o_ref, x_ref[0] + g_ref[0] * acc, pl.program_id(2), a_ref.shape[1])


def _ab_out(a, bm, w, x, gate, tm, tn):
    b, n, d = x.shape
    ka = a.shape[2]
    return pl.pallas_call(
        _ab_out_kernel,
        grid=(b, n // tm, d // tn),
        in_specs=[
            pl.BlockSpec((1, tm, ka), lambda bi, i, j: (bi, i, 0)),
            pl.BlockSpec((1, tm, ka), lambda bi, i, j: (bi, i, 0)),
            pl.BlockSpec((ka, tn), lambda bi, i, j: (0, j)),
            pl.BlockSpec((ka, tn), lambda bi, i, j: (1, j)),
            pl.BlockSpec((1, tm, tn), lambda bi, i, j: (bi, i, j)),
            pl.BlockSpec((1, 1, tn), lambda bi, i, j: (bi, 0, j)),
        ],
        out_specs=pl.BlockSpec((1, tm * PITCH, LANES), lambda bi, i, j: (bi, i, 0)),
        out_shape=jax.ShapeDtypeStruct((b, n * PITCH, LANES), F32),
        compiler_params=_params(("arbitrary", "arbitrary", "arbitrary")),
        name="ab_out_proj",
    )(a, bm, w, w, x, gate)


def _c_in_kernel(x_ref, gam_ref, sh_ref, sc_ref, wb_ref, wc_ref, wz_ref, bg_ref, cz_ref, h_ref, xf_ref):
    j = pl.program_id(2)

    @pl.when(j == 0)
    def _():
        _slab_load_rows(x_ref, (0,), h_ref.shape[0], xf_ref)
        h_ref[...] = _norm_mod(xf_ref[...], gam_ref[...], sh_ref[0], sc_ref[0]).astype(BF16)

    h = h_ref[...]
    bg_ref[0] = jnp.dot(h, wb_ref[...], preferred_element_type=F32).astype(bg_ref.dtype)
    cg = jnp.dot(h, wc_ref[...], preferred_element_type=F32)
    z = jnp.dot(h, wz_ref[...], preferred_element_type=F32)
    cz_ref[0] = (cg * z).astype(cz_ref.dtype)


def _c_in(x, gamma, shift, scale, w, tm, tn):
    b = x.shape[0]
    n = x.shape[1] // PITCH
    d = w.shape[0]
    cw = w.shape[1] // 3
    nj = cw // tn
    wspec = lambda seg: pl.BlockSpec((d, tn), lambda bi, i, j: (0, seg * nj + j))
    out = jax.ShapeDtypeStruct((b, n, cw), BF16)
    ospec = pl.BlockSpec((1, tm, tn), lambda bi, i, j: (bi, i, j))
    return pl.pallas_call(
        _c_in_kernel,
        grid=(b, n // tm, nj),
        in_specs=[
            pl.BlockSpec((1, tm * PITCH, LANES), lambda bi, i, j: (bi, i, 0)),
            pl.BlockSpec((1, d), lambda bi, i, j: (0, 0)),
            pl.BlockSpec((1, 1, d), lambda bi, i, j: (bi, 0, 0)),
            pl.BlockSpec((1, 1, d), lambda bi, i, j: (bi, 0, 0)),
            wspec(0), wspec(1), wspec(2),
        ],
        out_specs=[ospec, ospec],
        out_shape=[out, out],
        scratch_shapes=[pltpu.VMEM((tm, d), BF16), pltpu.VMEM((tm, d), F32)],
        compiler_params=_params(("arbitrary", "arbitrary", "arbitrary")),
        name="c_in_proj",
    )(x, gamma, shift, scale, w, w, w)


HALO = 8


def _c_out_kernel(bg_ref, cz_ref, czp_ref, czn_ref, cw_ref, w_ref, x_ref, g_ref, o_ref, lhs_ref, *, cchunk):
    i = pl.program_id(1)
    j = pl.program_id(2)
    ni = pl.num_programs(1)
    tm = cz_ref.shape[1]
    cw = cz_ref.shape[2]

    @pl.when(j == 0)
    def _():
        row = lax.broadcasted_iota(jnp.int32, (tm, cchunk), 0)
        for c in range(cw // cchunk):
            cols = slice(c * cchunk, (c + 1) * cchunk)
            cz = cz_ref[0, :, cols].astype(F32)
            prev_row = jnp.where(i > 0, czp_ref[0, HALO - 1:HALO, cols].astype(F32), 0.0)
            next_row = jnp.where(i < ni - 1, czn_ref[0, 0:1, cols].astype(F32), 0.0)
            up = jnp.where(row == 0, prev_row, pltpu.roll(cz, 1, 0))
            dn = jnp.where(row == tm - 1, next_row, pltpu.roll(cz, tm - 1, 0))
            y = cw_ref[0:1, cols] * up + cw_ref[1:2, cols] * cz + cw_ref[2:3, cols] * dn
            lhs_ref[:, cols] = (bg_ref[0, :, cols].astype(F32) * y).astype(BF16)

    acc = jnp.dot(lhs_ref[...], w_ref[...], preferred_element_type=F32)
    nc = acc.shape[1] // LANES
    x = jnp.concatenate([x_ref[_slab_cols(x_ref, (0,), j * nc + cc, tm)] for cc in range(nc)], axis=1)
    _store_slab_tile(o_ref, x + g_ref[0] * acc, j, tm)


def _c_out(bg, cz, conv_w, w, x, gate, tm, tn):
    b, n, cw = bg.shape
    d = w.shape[1]
    hb = tm // HALO
    nh = n // HALO
    return pl.pallas_call(
        functools.partial(_c_out_kernel, cchunk=512),
        grid=(b, n // tm, d // tn),
        in_specs=[
            pl.BlockSpec((1, tm, cw), lambda bi, i, j: (bi, i, 0)),
            pl.BlockSpec((1, tm, cw), lambda bi, i, j: (bi, i, 0)),
            pl.BlockSpec((1, HALO, cw), lambda bi, i, j: (bi, jnp.maximum(i * hb - 1, 0), 0)),
            pl.BlockSpec((1, HALO, cw), lambda bi, i, j: (bi, jnp.minimum((i + 1) * hb, nh - 1), 0)),
            pl.BlockSpec((CONV_W, cw), lambda bi, i, j: (0, 0)),
            pl.BlockSpec((cw, tn), lambda bi, i, j: (0, j)),
            pl.BlockSpec((1, tm * PITCH, LANES), lambda bi, i, j: (bi, i, 0)),
            pl.BlockSpec((1, 1, tn), lambda bi, i, j: (bi, 0, j)),
        ],
        out_specs=pl.BlockSpec((1, tm * PITCH, LANES), lambda bi, i, j: (bi, i, 0)),
        out_shape=jax.ShapeDtypeStruct((b, n * PITCH, LANES), F32),
        scratch_shapes=[pltpu.VMEM((tm, cw), BF16)],
        compiler_params=_params(("arbitrary", "arbitrary", "arbitrary")),
        name="c_out_proj",
    )(bg, cz, cz, cz, conv_w, w, x, gate)


def _router_kernel(x_ref, gam_ref, sh_ref, sc_ref, wr_ref, h_ref, aff_ref, xf_ref):
    tm = xf_ref.shape[0]
    _slab_load_rows(x_ref, (0,), tm, xf_ref)
    h = _norm_mod(xf_ref[...], gam_ref[...], sh_ref[0], sc_ref[0])
    _slab_zero_pad(h_ref, (0,), tm)
    for c in range(SLAB_ROWS):
        h_ref[_slab_cols(h_ref, (0,), c, tm)] = h[:, c * LANES:(c + 1) * LANES]
    logits = lax.dot_general(wr_ref[...], h, (((1,), (1,)), ((), ())), precision=HIGHEST,
                             preferred_element_type=F32)
    m = jnp.max(logits, axis=0, keepdims=True)
    e = jnp.exp(logits - m)
    aff_ref[0] = e / jnp.sum(e, axis=0, keepdims=True)


def _router(x, gamma, shift, scale, wr_t, tm):
    b = x.shape[0]
    n = x.shape[1] // PITCH
    ne, d = wr_t.shape
    slab = pl.BlockSpec((1, tm * PITCH, LANES), lambda bi, i: (bi, i, 0))
    return pl.pallas_call(
        _router_kernel,
        grid=(b, n // tm),
        in_specs=[
            slab,
            pl.BlockSpec((1, d), lambda bi, i: (0, 0)),
            pl.BlockSpec((1, 1, d), lambda bi, i: (bi, 0, 0)),
            pl.BlockSpec((1, 1, d), lambda bi, i: (bi, 0, 0)),
            pl.BlockSpec((ne, d), lambda bi, i: (0, 0)),
        ],
        out_specs=[slab, pl.BlockSpec((1, ne, tm), lambda bi, i: (bi, 0, i))],
        out_shape=[jax.ShapeDtypeStruct(x.shape, F32), jax.ShapeDtypeStruct((b, ne, n), F32)],
        scratch_shapes=[pltpu.VMEM((tm, d), F32)],
        compiler_params=_params(("arbitrary", "arbitrary")),
        name="moe_router",
    )(x, gamma, shift, scale, wr_t)


def _select_kernel(aff_ref, idx_ref, gate_ref, *, cap, cchunk):
    v = aff_ref[0, 0]
    nr = v.shape[0]

    def count(mask):
        return jnp.sum(jnp.sum(mask.astype(F32), axis=1, keepdims=True), axis=0, keepdims=True)

    def search(bit, t):
        cand = t | (jnp.int32(1) << (29 - bit))
        return jnp.where(count(v >= pltpu.bitcast(cand, F32)) >= cap, cand, t)

    thr = pltpu.bitcast(lax.fori_loop(0, 30, search, jnp.zeros((1, 1), jnp.int32)), F32)
    gt = v > thr
    eq = v == thr
    need = cap - count(gt)

    lane_l = lax.broadcasted_iota(jnp.int32, (LANES, LANES), 0)
    lane_c = lax.broadcasted_iota(jnp.int32, (LANES, LANES), 1)
    tri_lane = (lane_l <= lane_c).astype(BF16)
    row_r = lax.broadcasted_iota(jnp.int32, (nr, nr), 0)
    row_c = lax.broadcasted_iota(jnp.int32, (nr, nr), 1)
    tri_row = (row_c <= row_r).astype(BF16)

    def prefix(mask):
        mf = mask.astype(BF16)
        in_row = jnp.dot(mf, tri_lane, preferred_element_type=F32)
        colcum = jnp.dot(tri_row, mf, preferred_element_type=F32)
        row_incl = jnp.sum(colcum, axis=1, keepdims=True)
        row_tot = jnp.sum(mask.astype(F32), axis=1, keepdims=True)
        return in_row, row_incl - row_tot, row_incl

    eq_in, eq_off, _ = prefix(eq)
    eq_rank = eq_in + eq_off - eq.astype(F32)
    sel = gt | (eq & (eq_rank < need))
    _, sel_off, sel_incl = prefix(sel)

    self_bf = sel.astype(BF16)
    tri_lane_t = (lane_c <= lane_l).astype(BF16)
    dn_t = (((1,), (1,)), ((), ()))
    pt = lax.dot_general(tri_lane_t, self_bf, dn_t, preferred_element_type=F32)
    eye = (lane_l == lane_c).astype(F32)
    vt = lax.dot_general(eye, v, dn_t, precision=HIGHEST, preferred_element_type=F32)

    for c0 in range(0, cap, cchunk):
        cc = min(cchunk, cap - c0)
        slot = (lax.broadcasted_iota(jnp.int32, (1, cc), 1) + c0).astype(F32)
        r_of = jnp.sum((sel_incl <= slot).astype(F32), axis=0, keepdims=True)
        onehot = (lax.broadcasted_iota(jnp.int32, (nr, cc), 0).astype(F32) == r_of)
        onehot_f = onehot.astype(F32)
        local = slot - jnp.sum(onehot_f * sel_off, axis=0, keepdims=True)
        prow = jnp.dot(pt.astype(BF16), onehot.astype(BF16), preferred_element_type=F32)
        l_of = jnp.sum((prow <= local).astype(F32), axis=0, keepdims=True)
        vrow = jnp.dot(vt, onehot_f, precision=HIGHEST, preferred_element_type=F32)
        lane_i = lax.broadcasted_iota(jnp.int32, (LANES, cc), 0).astype(F32)
        gsel = jnp.sum(jnp.where(lane_i == l_of, vrow, 0.0), axis=0, keepdims=True)
        idx_ref[0, 0, :, c0:c0 + cc] = (r_of * LANES + l_of).astype(jnp.int32)
        gate_ref[0, 0, :, c0:c0 + cc] = gsel


def _select(aff_t, cap):
    b, ne, n = aff_t.shape
    nr = n // LANES
    out = lambda dt: jax.ShapeDtypeStruct((b, ne, 1, cap), dt)
    ospec = pl.BlockSpec((1, 1, 1, cap), lambda bi, e: (bi, e, 0, 0))
    idx, gate = pl.pallas_call(
        functools.partial(_select_kernel, cap=cap, cchunk=512),
        grid=(b, ne),
        in_specs=[pl.BlockSpec((1, 1, nr, LANES), lambda bi, e: (bi, e, 0, 0))],
        out_specs=[ospec, ospec],
        out_shape=[out(jnp.int32), out(F32)],
        compiler_params=_params(("arbitrary", "arbitrary")),
        name="moe_select",
    )(aff_t.reshape(b, ne, nr, LANES))
    return idx.reshape(b, ne, cap), gate.reshape(b, ne, cap)


X_SLOTS = 2
O_SLOTS = 3
ISSUE_UNROLL = 8


def _moe_kernel(idx_ref, idxn_ref, g_ref, g2_ref, h_hbm, wg_ref, wu_ref, wd_ref, x_hbm, o_hbm,
                xg, og, xs_ref, sem_x, sem_o, sem_s, *, n_tok):
    del x_hbm
    e, bi, t = pl.program_id(0), pl.program_id(1), pl.program_id(2)
    nb, nt = pl.num_programs(1), pl.num_programs(2)
    step = (e * nb + bi) * nt + t
    last = pl.num_programs(0) * nb * nt - 1
    tc = xs_ref.shape[0]
    moved = tc * SLAB_ROWS

    def for_each_slot_row(idx, sample, fn):
        def body(s8, carry):
            for u in range(ISSUE_UNROLL):
                s = s8 * ISSUE_UNROLL + u
                src = pl.multiple_of((sample * n_tok + idx[0, 0, s]) * PITCH, 4)
                fn(pl.ds(src, SLAB_ROWS), pl.ds(pl.multiple_of(s * PITCH, 4), SLAB_ROWS))
            return carry
        lax.fori_loop(0, tc // ISSUE_UNROLL, body, 0)

    def issue_gathers(idx, sample, xslot, oslot):
        def one(src, dst):
            pltpu.make_async_copy(h_hbm.at[src, :], xg.at[xslot, dst, :], sem_x.at[xslot]).start()
            pltpu.make_async_copy(o_hbm.at[src, :], og.at[oslot, dst, :], sem_o.at[oslot]).start()
        for_each_slot_row(idx, sample, one)

    def wait_rows(hbm, buf, slot, sem, to_hbm):
        a, b = hbm.at[pl.ds(0, moved), :], buf.at[slot, pl.ds(0, moved), :]
        (pltpu.make_async_copy(b, a, sem.at[slot]) if to_hbm else pltpu.make_async_copy(a, b, sem.at[slot])).wait()

    @pl.when(step == 0)
    def _():
        issue_gathers(idx_ref, bi, 0, 0)

    @pl.when(step >= 2)
    def _():
        wait_rows(o_hbm, og, (step - 2) % O_SLOTS, sem_s, True)

    @pl.when(step < last)
    def _():
        nxt = step + 1
        issue_gathers(idxn_ref, (nxt // nt) % nb, nxt % X_SLOTS, nxt % O_SLOTS)

    xslot = step % X_SLOTS
    oslot = step % O_SLOTS
    wait_rows(h_hbm, xg, xslot, sem_x, False)
    for c in range(SLAB_ROWS):
        xs_ref[:, c * LANES:(c + 1) * LANES] = xg[xslot, pl.ds(c, tc, stride=PITCH), :].astype(BF16)
    xs = xs_ref[...]
    gate = jnp.dot(xs, wg_ref[0], preferred_element_type=F32)
    up = jnp.dot(xs, wu_ref[0], preferred_element_type=F32)
    hid = (_silu(gate) * up).astype(BF16)
    y = jnp.dot(hid, wd_ref[0], preferred_element_type=F32)
    eye = lax.broadcasted_iota(jnp.int32, (tc, tc), 0) == lax.broadcasted_iota(jnp.int32, (tc, tc), 1)
    gcol = jnp.sum(jnp.where(eye, g_ref[0], 0.0), axis=1, keepdims=True)
    y = y * gcol

    wait_rows(o_hbm, og, oslot, sem_o, False)
    for c in range(SLAB_ROWS):
        cols = slice(c * LANES, (c + 1) * LANES)
        rows = (oslot, pl.ds(c, tc, stride=PITCH), slice(None))
        og[rows] = og[rows] + g2_ref[0][:, cols] * y[:, cols]

    def scatter(src, dst):
        pltpu.make_async_copy(og.at[oslot, dst, :], o_hbm.at[src, :], sem_s.at[oslot]).start()
    for_each_slot_row(idx_ref, bi, scatter)

    @pl.when(step == last)
    def _():
        @pl.when(step >= 1)
        def _():
            wait_rows(o_hbm, og, (step - 1) % O_SLOTS, sem_s, True)
        wait_rows(o_hbm, og, oslot, sem_s, True)


def _moe_experts(idx, gate, h, wg, wu, wd, gate2, x, tc):
    b, ne, cap = idx.shape
    n = x.shape[1] // PITCH
    d, f = wg.shape[1], wg.shape[2]
    nt = cap // tc
    assert b >= 2 and nt >= 2 and tc % ISSUE_UNROLL == 0 and d == SLAB_ROWS * LANES
    nsteps = ne * b * nt

    def cur(e, bi, t):
        return ((bi * ne + e) * nt + t, 0, 0)

    def nxt(e, bi, t):
        step = jnp.minimum((e * b + bi) * nt + t + 1, nsteps - 1)
        return (((step // nt) % b * ne + step // (nt * b)) * nt + step % nt, 0, 0)

    idx3 = idx.reshape(b * ne * nt, 1, tc)
    rows = (tc * PITCH, LANES)
    out = pl.pallas_call(
        functools.partial(_moe_kernel, n_tok=n),
        grid=(ne, b, nt),
        in_specs=[
            pl.BlockSpec((1, 1, tc), cur, memory_space=pltpu.SMEM),
            pl.BlockSpec((1, 1, tc), nxt, memory_space=pltpu.SMEM),
            pl.BlockSpec((1, 1, tc), cur),
            pl.BlockSpec((1, 1, d), lambda e, bi, t: (bi, 0, 0)),
            pl.BlockSpec(memory_space=pl.ANY),
            pl.BlockSpec((1, d, f), lambda e, bi, t: (e, 0, 0)),
            pl.BlockSpec((1, d, f), lambda e, bi, t: (e, 0, 0)),
            pl.BlockSpec((1, f, d), lambda e, bi, t: (e, 0, 0)),
            pl.BlockSpec(memory_space=pl.ANY),
        ],
        out_specs=pl.BlockSpec(memory_space=pl.ANY),
        out_shape=jax.ShapeDtypeStruct((b * n * PITCH, LANES), F32),
        input_output_aliases={8: 0},
        scratch_shapes=[
            pltpu.VMEM((X_SLOTS, *rows), F32), pltpu.VMEM((O_SLOTS, *rows), F32), pltpu.VMEM((tc, d), BF16),
            pltpu.SemaphoreType.DMA((X_SLOTS,)), pltpu.SemaphoreType.DMA((O_SLOTS,)),
            pltpu.SemaphoreType.DMA((O_SLOTS,)),
        ],
        compiler_params=_params(("arbitrary", "arbitrary", "arbitrary"), disable_bounds_checks=True),
        name="moe_experts",
    )(idx3, idx3, gate.reshape(b * ne * nt, 1, tc), gate2, h.reshape(b * n * PITCH, LANES), wg, wu, wd,
      x.reshape(b * n * PITCH, LANES))
    return out.reshape(x.shape)


def _ec_moe_residual(x, gamma, shift, scale, gate2, wr_t, wg, wu, wd, *, tm, tc):
    n = x.shape[1] // PITCH
    cap = CAP_FACTOR * n // N_EXPERTS
    h, aff_t = _router(x, gamma, shift, scale, wr_t, tm)
    idx, g = _select(aff_t, cap)
    return _moe_experts(idx, g, h, wg, wu, wd, gate2, x, tc)


def _slab_to_std_kernel(x_ref, o_ref):
    _slab_load_rows(x_ref, (0,), o_ref.shape[1], o_ref.at[0])


def _slab_to_std(x, d, tm):
    b = x.shape[0]
    n = x.shape[1] // PITCH
    return pl.pallas_call(
        _slab_to_std_kernel,
        grid=(b, n // tm),
        in_specs=[pl.BlockSpec((1, tm * PITCH, LANES), lambda bi, i: (bi, i, 0))],
        out_specs=pl.BlockSpec((1, tm, d), lambda bi, i: (bi, i, 0)),
        out_shape=jax.ShapeDtypeStruct((b, n, d), F32),
        compiler_params=_params(("arbitrary", "arbitrary")),
        name="slab_to_std",
    )(x)


def kernel(x, c, ctx, c_ctx, ada_w, ada_b, norm1_g, norm2_g, ab_w_in, ab_w_out, a_ws, a_bs, a_vnorm_g,
           b_qnorm_g, b_knorm_g, b_rpb, c_w_in, c_conv_w, c_w_out, router_w, moe_w_gate, moe_w_up, moe_w_down):
    bsz, seq, d = x.shape
    depth = ada_w.shape[0]

    cond = jnp.concatenate([c, c_ctx[None], jnp.zeros((8 - bsz - 1, d), F32)], axis=0)
    mod = _ada(cond, ada_w, ada_b)

    xl = x
    for i in range(depth):
        chunks = [mod[i, :bsz, k * d:(k + 1) * d].reshape(bsz, 1, d) for k in range(6)]
        sh1, sc1, g1, sh2, sc2, g2 = chunks
        cchunks = [jnp.broadcast_to(mod[i, bsz, k * d:(k + 1) * d].reshape(1, 1, d), (bsz, 1, d)) for k in range(6)]
        csh1, csc1 = cchunks[0], cchunks[1]
        gam1 = norm1_g[i].reshape(1, d)
        gam2 = norm2_g[i].reshape(1, d)
        j = i // 2
        if i % 2 == 0:
            w_in = ab_w_in[j].astype(BF16)
            w_out = ab_w_out[j].astype(BF16)
            ones = jnp.ones((A_WIDTH,), F32)
            gain = jnp.concatenate([
                ones, a_vnorm_g[j].reshape(-1),
                jnp.tile(b_qnorm_g[j], B_HEADS) * (B_DIM ** -0.5),
                jnp.tile(b_knorm_g[j], B_HEADS), ones]).reshape(1, -1)
            if i > 0:
                xl = _slab_to_std(xl, d, tm=512)
            p = _ab_in(xl, gam1, sh1, sc1, w_in, gain, tm=512)
            pc = _ab_in(ctx, gam1, csh1, csc1, w_in, gain, tm=ctx.shape[1])
            ws = a_ws[j].astype(BF16)
            bs_b = jnp.broadcast_to(a_bs[j][:, :, None], (A_GROUPS, CHUNK, A_DIM)).astype(F32)
            a_l = _gmlp(p, ws, bs_b, tm=512)
            b_l = _neighborhood_attention(p, pc, _na_bias_table(b_rpb[j]))
            xl = _ab_out(a_l, b_l, w_out, xl, g1, tm=512, tn=512)
        else:
            bg, cz = _c_in(xl, gam1, sh1, sc1, c_w_in[j].astype(BF16), tm=512, tn=512)
            xl = _c_out(bg, cz, c_conv_w[j], c_w_out[j].astype(BF16), xl, g1, tm=512, tn=512)

        xl = _ec_moe_residual(xl, gam2, sh2, sc2, g2, router_w[i].T,
                              moe_w_gate[i].astype(BF16), moe_w_up[i].astype(BF16), moe_w_down[i].astype(BF16),
                              tm=512, tc=256)
    return _slab_to_std(xl, d, tm=512)
```

```python
import functools
import math

import jax
import jax.numpy as jnp
from jax import lax
from jax.experimental import pallas as pl
from jax.experimental.pallas import tpu as pltpu

GRID_W = 64
CHUNK = 128
A_GROUPS = 8
A_DIM = 128
A_WIDTH = A_GROUPS * A_DIM
B_HEADS = 8
B_DIM = 128
B_WIDTH = B_HEADS * B_DIM
NA_ROWS = 8
NA_COLS = 16
CONV_W = 3
N_EXPERTS = 16
CAP_FACTOR = 2
EPS = 1e-6

LANES = 128
VMEM_LIMIT = 56 * 1024 * 1024

F32 = jnp.float32
BF16 = jnp.bfloat16
HIGHEST = lax.Precision.HIGHEST
MASK_VALUE = -1e30


SLAB_ROWS = 16
PITCH = 20


def _params(sem, **kw):
    return pltpu.CompilerParams(dimension_semantics=sem, vmem_limit_bytes=VMEM_LIMIT, **kw)


def _slab_cols(ref, lead, c, tm):
    return (*lead, pl.ds(c, tm, stride=PITCH), slice(None))


def _slab_zero_pad(ref, lead, tm):
    for c in range(SLAB_ROWS, PITCH):
        ref[_slab_cols(ref, lead, c, tm)] = jnp.zeros((tm, LANES), ref.dtype)


def _slab_load_rows(ref, lead, tm, dst_ref):
    for c in range(SLAB_ROWS):
        dst_ref[:, c * LANES:(c + 1) * LANES] = ref[_slab_cols(ref, lead, c, tm)]


def _silu(x):
    return x * (1.0 / (1.0 + jnp.exp(-x)))


def _gelu_tanh(x):
    return 0.5 * x * (1.0 + jnp.tanh(math.sqrt(2.0 / math.pi) * (x + 0.044715 * (x * x * x))))


def _norm_mod(x, gamma, shift, scale):
    ms = jnp.mean(x * x, axis=-1, keepdims=True)
    return (x * lax.rsqrt(ms + EPS) * gamma) * (1.0 + scale) + shift


def _ada_kernel(c_ref, w_ref, b_ref, o_ref):
    s = _silu(c_ref[...])
    o_ref[0] = jnp.dot(s, w_ref[0], precision=HIGHEST, preferred_element_type=F32) + b_ref[0]


def _ada(cond, ada_w, ada_b):
    depth, d, n6 = ada_w.shape
    tn = 1024
    return pl.pallas_call(
        _ada_kernel,
        grid=(depth, n6 // tn),
        in_specs=[
            pl.BlockSpec((8, d), lambda l, j: (0, 0)),
            pl.BlockSpec((1, d, tn), lambda l, j: (l, 0, j)),
            pl.BlockSpec((1, 1, tn), lambda l, j: (l, 0, j)),
        ],
        out_specs=pl.BlockSpec((1, 8, tn), lambda l, j: (l, 0, j)),
        out_shape=jax.ShapeDtypeStruct((depth, 8, n6), F32),
        compiler_params=_params(("arbitrary", "arbitrary")),
        name="ada_mod",
    )(cond, ada_w, ada_b.reshape(depth, 1, n6))


AB_SEGMENT_EPILOGUES = ("gelu", "gelu_norm", "norm", "norm", "none")


def _ab_in_kernel(x_ref, gam_ref, sh_ref, sc_ref, w_ref, gain_ref, o_ref, h_ref, *, tn):
    h_ref[...] = _norm_mod(x_ref[0], gam_ref[...], sh_ref[0], sc_ref[0]).astype(BF16)
    per_seg = A_WIDTH // tn
    for jt in range(w_ref.shape[1] // tn):
        kind = AB_SEGMENT_EPILOGUES[jt // per_seg]
        acc = jnp.dot(h_ref[...], w_ref[:, jt * tn:(jt + 1) * tn], preferred_element_type=F32)
        if kind.startswith("gelu"):
            acc = _gelu_tanh(acc)
        if kind.endswith("norm"):
            for g in range(tn // LANES):
                sl = slice(jt * tn + g * LANES, jt * tn + (g + 1) * LANES)
                ag = acc[:, g * LANES:(g + 1) * LANES]
                ms = jnp.mean(ag * ag, axis=-1, keepdims=True)
                o_ref[0, :, sl] = (ag * lax.rsqrt(ms + EPS) * gain_ref[:, sl]).astype(o_ref.dtype)
        else:
            o_ref[0, :, jt * tn:(jt + 1) * tn] = acc.astype(o_ref.dtype)


def _resident(shape):
    nd = len(shape)
    return pl.BlockSpec(shape, lambda *_: (0,) * nd, pipeline_mode=pl.Buffered(1))


def _ab_in(x, gamma, shift, scale, w, gain, tm, tn=512):
    b, n, d = x.shape
    f = w.shape[1]
    return pl.pallas_call(
        functools.partial(_ab_in_kernel, tn=tn),
        grid=(b, n // tm),
        in_specs=[
            pl.BlockSpec((1, tm, d), lambda bi, i: (bi, i, 0)),
            _resident((1, d)),
            pl.BlockSpec((1, 1, d), lambda bi, i: (bi, 0, 0)),
            pl.BlockSpec((1, 1, d), lambda bi, i: (bi, 0, 0)),
            _resident((d, f)),
            _resident((1, f)),
        ],
        out_specs=pl.BlockSpec((1, tm, f), lambda bi, i: (bi, i, 0)),
        out_shape=jax.ShapeDtypeStruct((b, n, f), BF16),
        scratch_shapes=[pltpu.VMEM((tm, d), BF16)],
        compiler_params=_params(("arbitrary", "arbitrary")),
        name="ab_in_proj",
    )(x, gamma, shift, scale, w, gain)


def _gmlp_kernel(u_ref, v_ref, ws_ref, bs_ref, o_ref):
    tm = u_ref.shape[1]
    for ch in range(tm // CHUNK):
        rows = slice(ch * CHUNK, (ch + 1) * CHUNK)
        for g in range(A_GROUPS):
            cols = slice(g * A_DIM, (g + 1) * A_DIM)
            s = jnp.dot(ws_ref[g], v_ref[0, rows, cols], preferred_element_type=F32) + bs_ref[g]
            o_ref[0, rows, cols] = (u_ref[0, rows, cols].astype(F32) * s).astype(o_ref.dtype)


def _gmlp(p, ws, bs_b, tm):
    b, n, _ = p.shape
    return pl.pallas_call(
        _gmlp_kernel,
        grid=(b, n // tm),
        in_specs=[
            pl.BlockSpec((1, tm, A_WIDTH), lambda bi, i: (bi, i, 0)),
            pl.BlockSpec((1, tm, A_WIDTH), lambda bi, i: (bi, i, 1)),
            pl.BlockSpec((A_GROUPS, CHUNK, CHUNK), lambda bi, i: (0, 0, 0)),
            pl.BlockSpec((A_GROUPS, CHUNK, A_DIM), lambda bi, i: (0, 0, 0)),
        ],
        out_specs=pl.BlockSpec((1, tm, A_WIDTH), lambda bi, i: (bi, i, 0)),
        out_shape=jax.ShapeDtypeStruct((b, n, A_WIDTH), BF16),
        compiler_params=_params(("arbitrary", "arbitrary")),
        name="gmlp",
    )(p, p, ws, bs_b)


ROWS_PER_STEP = NA_ROWS
WIN = NA_ROWS * GRID_W


def _na_bias_table(rpb):
    cols = jnp.arange(GRID_W)
    cstart = jnp.clip(cols - NA_COLS // 2, 0, GRID_W - NA_COLS)
    kc = jnp.arange(GRID_W)
    valid = (kc[None, :] >= cstart[:, None]) & (kc[None, :] < cstart[:, None] + NA_COLS)
    col_off = jnp.clip(kc[None, :] - cols[:, None] + (NA_COLS - 1), 0, 2 * NA_COLS - 2)
    row_off = jnp.arange(NA_ROWS)[None, :] - jnp.arange(NA_ROWS)[:, None] + (NA_ROWS - 1)
    t = rpb[:, row_off][:, :, :, col_off]
    t = jnp.where(valid[None, None, None], t, MASK_VALUE)
    t = jnp.transpose(t, (1, 0, 3, 2, 4))
    return t.reshape(NA_ROWS, rpb.shape[0], GRID_W, WIN).astype(F32)


def _na_kernel(q_ref, kp_ref, kc_ref, kn_ref, vp_ref, vc_ref, vn_ref, kx_ref, vx_ref, bias_ref, o_ref,
               kcat, vcat, *, rows):
    blk = pl.program_id(1)
    tq = ROWS_PER_STEP * GRID_W
    kcat[0:tq] = kp_ref[0]
    kcat[tq:2 * tq] = kc_ref[0]
    kcat[2 * tq:3 * tq] = kn_ref[0]
    vcat[0:tq] = vp_ref[0]
    vcat[tq:2 * tq] = vc_ref[0]
    vcat[2 * tq:3 * tq] = vn_ref[0]
    dn = (((1,), (1,)), ((), ()))

    def row_body(i, carry):
        r = blk * ROWS_PER_STEP + i
        rs = jnp.clip(r - NA_ROWS // 2, 0, rows - NA_ROWS)
        off = pl.multiple_of((rs - (blk - 1) * ROWS_PER_STEP) * GRID_W, GRID_W)
        var = r - rs
        qoff = pl.multiple_of(i * GRID_W, GRID_W)
        for h in range(B_HEADS):
            cols = slice(h * B_DIM, (h + 1) * B_DIM)
            q = q_ref[0, pl.ds(qoff, GRID_W), cols]
            kw = kcat[pl.ds(off, WIN), cols]
            vw = vcat[pl.ds(off, WIN), cols]
            s_loc = lax.dot_general(q, kw, dn, preferred_element_type=F32) + bias_ref[var, h]
            s_ctx = lax.dot_general(q, kx_ref[0, :, cols], dn, preferred_element_type=F32)
            m = jnp.maximum(jnp.max(s_loc, axis=-1, keepdims=True), jnp.max(s_ctx, axis=-1, keepdims=True))
            p_loc = jnp.exp(s_loc - m)
            p_ctx = jnp.exp(s_ctx - m)
            l = jnp.sum(p_loc, axis=-1, keepdims=True) + jnp.sum(p_ctx, axis=-1, keepdims=True)
            o = (jnp.dot(p_loc.astype(BF16), vw, preferred_element_type=F32)
                 + jnp.dot(p_ctx.astype(BF16), vx_ref[0, :, cols], preferred_element_type=F32))
            o_ref[0, pl.ds(qoff, GRID_W), cols] = (o / l).astype(o_ref.dtype)
        return carry

    lax.fori_loop(0, ROWS_PER_STEP, row_body, 0)


def _neighborhood_attention(p, pc, bias):
    b, s, _ = p.shape
    lc = pc.shape[1]
    rows = s // GRID_W
    tq = ROWS_PER_STEP * GRID_W
    nblk = s // tq
    qcol, kcol, vcol = 2, 3, 4
    spec = lambda col, fn: pl.BlockSpec((1, tq, B_WIDTH), lambda bi, i: (bi, fn(i), col))
    prev = lambda i: jnp.maximum(i - 1, 0)
    cur = lambda i: i
    nxt = lambda i: jnp.minimum(i + 1, nblk - 1)
    return pl.pallas_call(
        functools.partial(_na_kernel, rows=rows),
        grid=(b, nblk),
        in_specs=[
            spec(qcol, cur),
            spec(kcol, prev), spec(kcol, cur), spec(kcol, nxt),
            spec(vcol, prev), spec(vcol, cur), spec(vcol, nxt),
            pl.BlockSpec((1, lc, B_WIDTH), lambda bi, i: (bi, 0, kcol)),
            pl.BlockSpec((1, lc, B_WIDTH), lambda bi, i: (bi, 0, vcol)),
            pl.BlockSpec(bias.shape, lambda bi, i: (0, 0, 0, 0)),
        ],
        out_specs=pl.BlockSpec((1, tq, B_WIDTH), lambda bi, i: (bi, i, 0)),
        out_shape=jax.ShapeDtypeStruct((b, s, B_WIDTH), BF16),
        scratch_shapes=[pltpu.VMEM((3 * tq, B_WIDTH), BF16), pltpu.VMEM((3 * tq, B_WIDTH), BF16)],
        compiler_params=_params(("arbitrary", "arbitrary")),
        name="neighborhood_attention",
    )(p, p, p, p, p, p, p, pc, pc, bias)


def _store_residual_and_route(xf_ref, gam_ref, sh_ref, sc_ref, wr_ref, o_ref, h_ref, aff_ref):
    tm = xf_ref.shape[0]
    xn = xf_ref[...]
    h = _norm_mod(xn, gam_ref[...], sh_ref[0], sc_ref[0])
    _slab_zero_pad(o_ref, (0,), tm)
    _slab_zero_pad(h_ref, (0,), tm)
    for c in range(SLAB_ROWS):
        sl = slice(c * LANES, (c + 1) * LANES)
        o_ref[_slab_cols(o_ref, (0,), c, tm)] = xn[:, sl]
        h_ref[_slab_cols(h_ref, (0,), c, tm)] = h[:, sl]
    ne = aff_ref.shape[1]
    h_hi = h.astype(BF16)
    h_lo = (h - h_hi.astype(F32)).astype(BF16)
    p_hi = jnp.dot(h_hi, wr_ref[...], preferred_element_type=F32).T
    p_lo = jnp.dot(h_lo, wr_ref[...], preferred_element_type=F32).T
    logits = p_hi[0:ne] + (p_hi[ne:2 * ne] + p_lo[0:ne])
    m = jnp.max(logits, axis=0, keepdims=True)
    e = jnp.exp(logits - m)
    aff_ref[0] = e / jnp.sum(e, axis=0, keepdims=True)


def _router_weight(router_w):
    d, ne = router_w.shape
    w_hi = router_w.astype(BF16)
    w_lo = (router_w - w_hi.astype(F32)).astype(BF16)
    return jnp.concatenate([w_hi, w_lo, jnp.zeros((d, LANES - 2 * ne), BF16)], axis=1)


def _route_specs(b, n, d, ne, tm):
    slab = pl.BlockSpec((1, tm * PITCH, LANES), lambda bi, i: (bi, i, 0))
    mod = pl.BlockSpec((1, 1, d), lambda bi, i: (bi, 0, 0))
    in_specs = [_resident((1, d)), mod, mod, _resident((d, LANES))]
    out_specs = [slab, slab, pl.BlockSpec((1, ne, tm), lambda bi, i: (bi, 0, i))]
    slab_shape = jax.ShapeDtypeStruct((b, n * PITCH, LANES), F32)
    return in_specs, out_specs, [slab_shape, slab_shape, jax.ShapeDtypeStruct((b, ne, n), F32)]


def _ab_out_kernel(a_ref, b_ref, w_ref, x_ref, g_ref, gam_ref, sh_ref, sc_ref, wr_ref, o_ref, h_ref, aff_ref,
                   xf_ref, *, tn):
    ka = a_ref.shape[2]
    for jt in range(w_ref.shape[1] // tn):
        cols = slice(jt * tn, (jt + 1) * tn)
        acc = jnp.dot(a_ref[0], w_ref[0:ka, cols], preferred_element_type=F32)
        acc += jnp.dot(b_ref[0], w_ref[ka:2 * ka, cols], preferred_element_type=F32)
        xf_ref[:, cols] = x_ref[0, :, cols] + g_ref[0, :, cols] * acc
    _store_residual_and_route(xf_ref, gam_ref, sh_ref, sc_ref, wr_ref, o_ref, h_ref, aff_ref)


def _ab_out(a, bm, w, x, gate, gamma2, shift2, scale2, wr_split, tm, tn=512):
    b, n, d = x.shape
    ka = a.shape[2]
    rin, rout, rshape = _route_specs(b, n, d, N_EXPERTS, tm)
    return pl.pallas_call(
        functools.partial(_ab_out_kernel, tn=tn),
        grid=(b, n // tm),
        in_specs=[
            pl.BlockSpec((1, tm, ka), lambda bi, i: (bi, i, 0)),
            pl.BlockSpec((1, tm, ka), lambda bi, i: (bi, i, 0)),
            _resident(w.shape),
            pl.BlockSpec((1, tm, d), lambda bi, i: (bi, i, 0)),
            pl.BlockSpec((1, 1, d), lambda bi, i: (bi, 0, 0)),
            *rin,
        ],
        out_specs=rout,
        out_shape=rshape,
        scratch_shapes=[pltpu.VMEM((tm, d), F32)],
        compiler_params=_params(("arbitrary", "arbitrary")),
        name="ab_out_proj",
    )(a, bm, w, x, gate, gamma2, shift2, scale2, wr_split)


def _c_in_kernel(x_ref, gam_ref, sh_ref, sc_ref, w_ref, bg_ref, cz_ref, h_ref, xf_ref, *, tn):
    _slab_load_rows(x_ref, (0,), h_ref.shape[0], xf_ref)
    h_ref[...] = _norm_mod(xf_ref[...], gam_ref[...], sh_ref[0], sc_ref[0]).astype(BF16)
    cw = w_ref.shape[1] // 3
    for jt in range(cw // tn):
        cols = slice(jt * tn, (jt + 1) * tn)
        proj = lambda seg: jnp.dot(h_ref[...], w_ref[:, seg * cw + jt * tn:seg * cw + (jt + 1) * tn],
                                   preferred_element_type=F32)
        bg_ref[0, :, cols] = proj(0).astype(bg_ref.dtype)
        cz_ref[0, :, cols] = (proj(1) * proj(2)).astype(cz_ref.dtype)


def _c_in(x, gamma, shift, scale, w, tm, tn=512):
    b = x.shape[0]
    n = x.shape[1] // PITCH
    d = w.shape[0]
    cw = w.shape[1] // 3
    out = jax.ShapeDtypeStruct((b, n, cw), BF16)
    ospec = pl.BlockSpec((1, tm, cw), lambda bi, i: (bi, i, 0))
    return pl.pallas_call(
        functools.partial(_c_in_kernel, tn=tn),
        grid=(b, n // tm),
        in_specs=[
            pl.BlockSpec((1, tm * PITCH, LANES), lambda bi, i: (bi, i, 0)),
            _resident((1, d)),
            pl.BlockSpec((1, 1, d), lambda bi, i: (bi, 0, 0)),
            pl.BlockSpec((1, 1, d), lambda bi, i: (bi, 0, 0)),
            _resident(w.shape),
        ],
        out_specs=[ospec, ospec],
        out_shape=[out, out],
        scratch_shapes=[pltpu.VMEM((tm, d), BF16), pltpu.VMEM((tm, d), F32)],
        compiler_params=_params(("arbitrary", "arbitrary")),
        name="c_in_proj",
    )(x, gamma, shift, scale, w)


HALO = 8


def _c_out_kernel(bg_ref, cz_ref, czp_ref, czn_ref, cw_ref, w_ref, x_ref, g_ref, gam_ref, sh_ref, sc_ref, wr_ref,
                  o_ref, h_ref, aff_ref, xf_ref, acc_ref, *, kchunk, tn):
    i = pl.program_id(1)
    ni = pl.num_programs(1)
    tm = cz_ref.shape[1]
    cw = cz_ref.shape[2]
    _slab_load_rows(x_ref, (0,), tm, xf_ref)
    row = lax.broadcasted_iota(jnp.int32, (tm, kchunk), 0)
    for kc in range(cw // kchunk):
        ks = slice(kc * kchunk, (kc + 1) * kchunk)
        cz = cz_ref[0, :, ks].astype(F32)
        prev_row = jnp.where(i > 0, czp_ref[0, HALO - 1:HALO, ks].astype(F32), 0.0)
        next_row = jnp.where(i < ni - 1, czn_ref[0, 0:1, ks].astype(F32), 0.0)
        up = jnp.where(row == 0, prev_row, pltpu.roll(cz, 1, 0))
        dn = jnp.where(row == tm - 1, next_row, pltpu.roll(cz, tm - 1, 0))
        y = cw_ref[0:1, ks] * up + cw_ref[1:2, ks] * cz + cw_ref[2:3, ks] * dn
        lhs = (bg_ref[0, :, ks].astype(F32) * y).astype(BF16)
        for jt in range(w_ref.shape[1] // tn):
            cols = slice(jt * tn, (jt + 1) * tn)
            part = jnp.dot(lhs, w_ref[ks, cols], preferred_element_type=F32)
            if kc == 0:
                acc_ref[:, cols] = part
            else:
                acc_ref[:, cols] += part
    xf_ref[...] = xf_ref[...] + g_ref[0] * acc_ref[...]
    _store_residual_and_route(xf_ref, gam_ref, sh_ref, sc_ref, wr_ref, o_ref, h_ref, aff_ref)


def _c_out(bg, cz, conv_w, w, x, gate, gamma2, shift2, scale2, wr_split, tm, tn=512):
    b, n, cw = bg.shape
    d = w.shape[1]
    hb = tm // HALO
    nh = n // HALO
    rin, rout, rshape = _route_specs(b, n, d, N_EXPERTS, tm)
    return pl.pallas_call(
        functools.partial(_c_out_kernel, kchunk=512, tn=tn),
        grid=(b, n // tm),
        in_specs=[
            pl.BlockSpec((1, tm, cw), lambda bi, i: (bi, i, 0)),
            pl.BlockSpec((1, tm, cw), lambda bi, i: (bi, i, 0)),
            pl.BlockSpec((1, HALO, cw), lambda bi, i: (bi, jnp.maximum(i * hb - 1, 0), 0)),
            pl.BlockSpec((1, HALO, cw), lambda bi, i: (bi, jnp.minimum((i + 1) * hb, nh - 1), 0)),
            _resident((CONV_W, cw)),
            _resident(w.shape),
            pl.BlockSpec((1, tm * PITCH, LANES), lambda bi, i: (bi, i, 0)),
            pl.BlockSpec((1, 1, d), lambda bi, i: (bi, 0, 0)),
            *rin,
        ],
        out_specs=rout,
        out_shape=rshape,
        scratch_shapes=[pltpu.VMEM((tm, d), F32), pltpu.VMEM((tm, d), F32)],
        compiler_params=_params(("arbitrary", "arbitrary")),
        name="c_out_proj",
    )(bg, cz, cz, cz, conv_w, w, x, gate, gamma2, shift2, scale2, wr_split)


def _select_kernel(aff_ref, idx_ref, gate_ref, *, cap, cchunk):
    v = aff_ref[0, 0]
    nr = v.shape[0]

    def count(mask):
        return jnp.sum(jnp.sum(mask.astype(F32), axis=1, keepdims=True), axis=0, keepdims=True)

    def search(bit, t):
        cand = t | (jnp.int32(1) << (29 - bit))
        return jnp.where(count(v >= pltpu.bitcast(cand, F32)) >= cap, cand, t)

    thr = pltpu.bitcast(lax.fori_loop(0, 30, search, jnp.zeros((1, 1), jnp.int32)), F32)
    gt = v > thr
    eq = v == thr
    need = cap - count(gt)

    lane_l = lax.broadcasted_iota(jnp.int32, (LANES, LANES), 0)
    lane_c = lax.broadcasted_iota(jnp.int32, (LANES, LANES), 1)
    tri_lane = (lane_l <= lane_c).astype(BF16)
    row_r = lax.broadcasted_iota(jnp.int32, (nr, nr), 0)
    row_c = lax.broadcasted_iota(jnp.int32, (nr, nr), 1)
    tri_row = (row_c <= row_r).astype(BF16)

    def prefix(mask):
        mf = mask.astype(BF16)
        in_row = jnp.dot(mf, tri_lane, preferred_element_type=F32)
        colcum = jnp.dot(tri_row, mf, preferred_element_type=F32)
        row_incl = jnp.sum(colcum, axis=1, keepdims=True)
        row_tot = jnp.sum(mask.astype(F32), axis=1, keepdims=True)
        return in_row, row_incl - row_tot, row_incl

    eq_in, eq_off, _ = prefix(eq)
    eq_rank = eq_in + eq_off - eq.astype(F32)
    sel = gt | (eq & (eq_rank < need))
    _, sel_off, sel_incl = prefix(sel)

    self_bf = sel.astype(BF16)
    tri_lane_t = (lane_c <= lane_l).astype(BF16)
    dn_t = (((1,), (1,)), ((), ()))
    pt = lax.dot_general(tri_lane_t, self_bf, dn_t, preferred_element_type=F32)
    eye = (lane_l == lane_c).astype(F32)
    vt = lax.dot_general(eye, v, dn_t, precision=HIGHEST, preferred_element_type=F32)

    for c0 in range(0, cap, cchunk):
        cc = min(cchunk, cap - c0)
        slot = (lax.broadcasted_iota(jnp.int32, (1, cc), 1) + c0).astype(F32)
        r_of = jnp.sum((sel_incl <= slot).astype(F32), axis=0, keepdims=True)
        onehot = (lax.broadcasted_iota(jnp.int32, (nr, cc), 0).astype(F32) == r_of)
        onehot_f = onehot.astype(F32)
        local = slot - jnp.sum(onehot_f * sel_off, axis=0, keepdims=True)
        prow = jnp.dot(pt.astype(BF16), onehot.astype(BF16), preferred_element_type=F32)
        l_of = jnp.sum((prow <= local).astype(F32), axis=0, keepdims=True)
        vrow = jnp.dot(vt, onehot_f, precision=HIGHEST, preferred_element_type=F32)
        lane_i = lax.broadcasted_iota(jnp.int32, (LANES, cc), 0).astype(F32)
        gsel = jnp.sum(jnp.where(lane_i == l_of, vrow, 0.0), axis=0, keepdims=True)
        idx_ref[0, 0, :, c0:c0 + cc] = (r_of * LANES + l_of).astype(jnp.int32)
        gate_ref[0, 0, :, c0:c0 + cc] = gsel


def _select(aff_t, cap):
    b, ne, n = aff_t.shape
    nr = n // LANES
    out = lambda dt: jax.ShapeDtypeStruct((b, ne, 1, cap), dt)
    ospec = pl.BlockSpec((1, 1, 1, cap), lambda bi, e: (bi, e, 0, 0))
    idx, gate = pl.pallas_call(
        functools.partial(_select_kernel, cap=cap, cchunk=512),
        grid=(b, ne),
        in_specs=[pl.BlockSpec((1, 1, nr, LANES), lambda bi, e: (bi, e, 0, 0))],
        out_specs=[ospec, ospec],
        out_shape=[out(jnp.int32), out(F32)],
        compiler_params=_params(("arbitrary", "arbitrary")),
        name="moe_select",
    )(aff_t.reshape(b, ne, nr, LANES))
    return idx.reshape(b, ne, cap), gate.reshape(b, ne, cap)


X_SLOTS = 2
O_SLOTS = 3
ISSUE_UNROLL = 8


def _moe_kernel(idx_ref, idxn_ref, g_ref, g2_ref, h_hbm, wg_ref, wu_ref, wd_ref, x_hbm, o_hbm,
                xg, og, xs_ref, sem_x, sem_o, sem_s, *, n_tok):
    del x_hbm
    e, bi, t = pl.program_id(0), pl.program_id(1), pl.program_id(2)
    nb, nt = pl.num_programs(1), pl.num_programs(2)
    step = (e * nb + bi) * nt + t
    last = pl.num_programs(0) * nb * nt - 1
    tc = xs_ref.shape[0]
    moved = tc * SLAB_ROWS

    def for_each_slot_row(idx, sample, fn):
        def body(s8, carry):
            for u in range(ISSUE_UNROLL):
                s = s8 * ISSUE_UNROLL + u
                src = pl.multiple_of((sample * n_tok + idx[0, 0, s]) * PITCH, 4)
                fn(pl.ds(src, SLAB_ROWS), pl.ds(pl.multiple_of(s * PITCH, 4), SLAB_ROWS))
            return carry
        lax.fori_loop(0, tc // ISSUE_UNROLL, body, 0)

    def issue_gathers(idx, sample, xslot, oslot):
        def one(src, dst):
            pltpu.make_async_copy(h_hbm.at[src, :], xg.at[xslot, dst, :], sem_x.at[xslot]).start()
            pltpu.make_async_copy(o_hbm.at[src, :], og.at[oslot, dst, :], sem_o.at[oslot]).start()
        for_each_slot_row(idx, sample, one)

    def wait_rows(hbm, buf, slot, sem, to_hbm):
        a, b = hbm.at[pl.ds(0, moved), :], buf.at[slot, pl.ds(0, moved), :]
        (pltpu.make_async_copy(b, a, sem.at[slot]) if to_hbm else pltpu.make_async_copy(a, b, sem.at[slot])).wait()

    @pl.when(step == 0)
    def _():
        issue_gathers(idx_ref, bi, 0, 0)

    @pl.when(step >= 2)
    def _():
        wait_rows(o_hbm, og, (step - 2) % O_SLOTS, sem_s, True)

    @pl.when(step < last)
    def _():
        nxt = step + 1
        issue_gathers(idxn_ref, (nxt // nt) % nb, nxt % X_SLOTS, nxt % O_SLOTS)

    xslot = step % X_SLOTS
    oslot = step % O_SLOTS
    wait_rows(h_hbm, xg, xslot, sem_x, False)
    for c in range(SLAB_ROWS):
        xs_ref[:, c * LANES:(c + 1) * LANES] = xg[xslot, pl.ds(c, tc, stride=PITCH), :].astype(BF16)
    xs = xs_ref[...]
    gate = jnp.dot(xs, wg_ref[0], preferred_element_type=F32)
    up = jnp.dot(xs, wu_ref[0], preferred_element_type=F32)
    hid = (_silu(gate) * up).astype(BF16)
    y = jnp.dot(hid, wd_ref[0], preferred_element_type=F32)
    eye = lax.broadcasted_iota(jnp.int32, (tc, tc), 0) == lax.broadcasted_iota(jnp.int32, (tc, tc), 1)
    gcol = jnp.sum(jnp.where(eye, g_ref[0], 0.0), axis=1, keepdims=True)
    y = y * gcol

    wait_rows(o_hbm, og, oslot, sem_o, False)
    for c in range(SLAB_ROWS):
        cols = slice(c * LANES, (c + 1) * LANES)
        rows = (oslot, pl.ds(c, tc, stride=PITCH), slice(None))
        og[rows] = og[rows] + g2_ref[0][:, cols] * y[:, cols]

    def scatter(src, dst):
        pltpu.make_async_copy(og.at[oslot, dst, :], o_hbm.at[src, :], sem_s.at[oslot]).start()
    for_each_slot_row(idx_ref, bi, scatter)

    @pl.when(step == last)
    def _():
        @pl.when(step >= 1)
        def _():
            wait_rows(o_hbm, og, (step - 1) % O_SLOTS, sem_s, True)
        wait_rows(o_hbm, og, oslot, sem_s, True)


def _moe_experts(idx, gate, h, wg, wu, wd, gate2, x, tc):
    b, ne, cap = idx.shape
    n = x.shape[1] // PITCH
    d, f = wg.shape[1], wg.shape[2]
    nt = cap // tc
    assert b >= 2 and nt >= 2 and tc % ISSUE_UNROLL == 0 and d == SLAB_ROWS * LANES
    nsteps = ne * b * nt

    def cur(e, bi, t):
        return ((bi * ne + e) * nt + t, 0, 0)

    def nxt(e, bi, t):
        step = jnp.minimum((e * b + bi) * nt + t + 1, nsteps - 1)
        return (((step // nt) % b * ne + step // (nt * b)) * nt + step % nt, 0, 0)

    idx3 = idx.reshape(b * ne * nt, 1, tc)
    rows = (tc * PITCH, LANES)
    out = pl.pallas_call(
        functools.partial(_moe_kernel, n_tok=n),
        grid=(ne, b, nt),
        in_specs=[
            pl.BlockSpec((1, 1, tc), cur, memory_space=pltpu.SMEM),
            pl.BlockSpec((1, 1, tc), nxt, memory_space=pltpu.SMEM),
            pl.BlockSpec((1, 1, tc), cur),
            pl.BlockSpec((1, 1, d), lambda e, bi, t: (bi, 0, 0)),
            pl.BlockSpec(memory_space=pl.ANY),
            pl.BlockSpec((1, d, f), lambda e, bi, t: (e, 0, 0)),
            pl.BlockSpec((1, d, f), lambda e, bi, t: (e, 0, 0)),
            pl.BlockSpec((1, f, d), lambda e, bi, t: (e, 0, 0)),
            pl.BlockSpec(memory_space=pl.ANY),
        ],
        out_specs=pl.BlockSpec(memory_space=pl.ANY),
        out_shape=jax.ShapeDtypeStruct((b * n * PITCH, LANES), F32),
        input_output_aliases={8: 0},
        scratch_shapes=[
            pltpu.VMEM((X_SLOTS, *rows), F32), pltpu.VMEM((O_SLOTS, *rows), F32), pltpu.VMEM((tc, d), BF16),
            pltpu.SemaphoreType.DMA((X_SLOTS,)), pltpu.SemaphoreType.DMA((O_SLOTS,)),
            pltpu.SemaphoreType.DMA((O_SLOTS,)),
        ],
        compiler_params=_params(("arbitrary", "arbitrary", "arbitrary"), disable_bounds_checks=True),
        name="moe_experts",
    )(idx3, idx3, gate.reshape(b * ne * nt, 1, tc), gate2, h.reshape(b * n * PITCH, LANES), wg, wu, wd,
      x.reshape(b * n * PITCH, LANES))
    return out.reshape(x.shape)


def _ec_moe_residual(x, h, aff_t, gate2, wg, wu, wd, *, tc):
    n = x.shape[1] // PITCH
    cap = CAP_FACTOR * n // N_EXPERTS
    idx, g = _select(aff_t, cap)
    return _moe_experts(idx, g, h, wg, wu, wd, gate2, x, tc)


def _slab_to_std_kernel(x_ref, o_ref):
    _slab_load_rows(x_ref, (0,), o_ref.shape[1], o_ref.at[0])


def _slab_to_std(x, d, tm):
    b = x.shape[0]
    n = x.shape[1] // PITCH
    return pl.pallas_call(
        _slab_to_std_kernel,
        grid=(b, n // tm),
        in_specs=[pl.BlockSpec((1, tm * PITCH, LANES), lambda bi, i: (bi, i, 0))],
        out_specs=pl.BlockSpec((1, tm, d), lambda bi, i: (bi, i, 0)),
        out_shape=jax.ShapeDtypeStruct((b, n, d), F32),
        compiler_params=_params(("arbitrary", "arbitrary")),
        name="slab_to_std",
    )(x)


def kernel(x, c, ctx, c_ctx, ada_w, ada_b, norm1_g, norm2_g, ab_w_in, ab_w_out, a_ws, a_bs, a_vnorm_g,
           b_qnorm_g, b_knorm_g, b_rpb, c_w_in, c_conv_w, c_w_out, router_w, moe_w_gate, moe_w_up, moe_w_down):
    bsz, seq, d = x.shape
    depth = ada_w.shape[0]

    cond = jnp.concatenate([c, c_ctx[None], jnp.zeros((8 - bsz - 1, d), F32)], axis=0)
    mod = _ada(cond, ada_w, ada_b)

    xl = x
    for i in range(depth):
        chunks = [mod[i, :bsz, k * d:(k + 1) * d].reshape(bsz, 1, d) for k in range(6)]
        sh1, sc1, g1, sh2, sc2, g2 = chunks
        cchunks = [jnp.broadcast_to(mod[i, bsz, k * d:(k + 1) * d].reshape(1, 1, d), (bsz, 1, d)) for k in range(6)]
        csh1, csc1 = cchunks[0], cchunks[1]
        gam1 = norm1_g[i].reshape(1, d)
        gam2 = norm2_g[i].reshape(1, d)
        wr_split = _router_weight(router_w[i])
        j = i // 2
        if i % 2 == 0:
            w_in = ab_w_in[j].astype(BF16)
            w_out = ab_w_out[j].astype(BF16)
            ones = jnp.ones((A_WIDTH,), F32)
            gain = jnp.concatenate([
                ones, a_vnorm_g[j].reshape(-1),
                jnp.tile(b_qnorm_g[j], B_HEADS) * (B_DIM ** -0.5),
                jnp.tile(b_knorm_g[j], B_HEADS), ones]).reshape(1, -1)
            if i > 0:
                xl = _slab_to_std(xl, d, tm=512)
            p = _ab_in(xl, gam1, sh1, sc1, w_in, gain, tm=512)
            pc = _ab_in(ctx, gam1, csh1, csc1, w_in, gain, tm=ctx.shape[1])
            ws = a_ws[j].astype(BF16)
            bs_b = jnp.broadcast_to(a_bs[j][:, :, None], (A_GROUPS, CHUNK, A_DIM)).astype(F32)
            a_l = _gmlp(p, ws, bs_b, tm=512)
            b_l = _neighborhood_attention(p, pc, _na_bias_table(b_rpb[j]))
            xl, h2, aff_t = _ab_out(a_l, b_l, w_out, xl, g1, gam2, sh2, sc2, wr_split, tm=512)
        else:
            bg, cz = _c_in(xl, gam1, sh1, sc1, c_w_in[j].astype(BF16), tm=512)
            xl, h2, aff_t = _c_out(bg, cz, c_conv_w[j], c_w_out[j].astype(BF16), xl, g1, gam2, sh2, sc2,
                                   wr_split, tm=512)

        xl = _ec_moe_residual(xl, h2, aff_t, g2, moe_w_gate[i].astype(BF16), moe_w_up[i].astype(BF16),
                              moe_w_down[i].astype(BF16), tc=256)
    return _slab_to_std(xl, d, tm=512)
```

```python
import functools
import math

import jax
import jax.numpy as jnp
import numpy as np
from jax import lax
from jax.experimental import pallas as pl
from jax.experimental.pallas import tpu as pltpu

GRID_W = 64
CHUNK = 128
A_GROUPS = 8
A_DIM = 128
A_WIDTH = A_GROUPS * A_DIM
B_HEADS = 8
B_DIM = 128
B_WIDTH = B_HEADS * B_DIM
NA_ROWS = 8
NA_COLS = 16
CONV_W = 3
N_EXPERTS = 16
CAP_FACTOR = 2
EPS = 1e-6

LANES = 128
VMEM_LIMIT = 56 * 1024 * 1024

F32 = jnp.float32
BF16 = jnp.bfloat16
HIGHEST = lax.Precision.HIGHEST
MASK_VALUE = -1e30


SLAB_ROWS = 16
PITCH = 20


def _params(sem, **kw):
    return pltpu.CompilerParams(dimension_semantics=sem, vmem_limit_bytes=VMEM_LIMIT, **kw)


def _slab_cols(ref, lead, c, tm):
    return (*lead, pl.ds(c, tm, stride=PITCH), slice(None))


def _slab_zero_pad(ref, lead, tm):
    for c in range(SLAB_ROWS, PITCH):
        ref[_slab_cols(ref, lead, c, tm)] = jnp.zeros((tm, LANES), ref.dtype)


def _slab_load_rows(ref, lead, tm, dst_ref):
    for c in range(SLAB_ROWS):
        dst_ref[:, c * LANES:(c + 1) * LANES] = ref[_slab_cols(ref, lead, c, tm)]


def _silu(x):
    return x * (1.0 / (1.0 + jnp.exp(-x)))


def _gelu_tanh(x):
    return 0.5 * x * (1.0 + jnp.tanh(math.sqrt(2.0 / math.pi) * (x + 0.044715 * (x * x * x))))


def _norm_mod(x, gamma, shift, scale):
    ms = jnp.mean(x * x, axis=-1, keepdims=True)
    return (x * lax.rsqrt(ms + EPS) * gamma) * (1.0 + scale) + shift


def _ada_kernel(c_ref, w_ref, b_ref, o_ref):
    s = _silu(c_ref[...])
    o_ref[0] = jnp.dot(s, w_ref[0], precision=HIGHEST, preferred_element_type=F32) + b_ref[0]


def _ada(cond, ada_w, ada_b):
    depth, d, n6 = ada_w.shape
    tn = 1024
    return pl.pallas_call(
        _ada_kernel,
        grid=(depth, n6 // tn),
        in_specs=[
            pl.BlockSpec((8, d), lambda l, j: (0, 0)),
            pl.BlockSpec((1, d, tn), lambda l, j: (l, 0, j)),
            pl.BlockSpec((1, 1, tn), lambda l, j: (l, 0, j)),
        ],
        out_specs=pl.BlockSpec((1, 8, tn), lambda l, j: (l, 0, j)),
        out_shape=jax.ShapeDtypeStruct((depth, 8, n6), F32),
        compiler_params=_params(("arbitrary", "arbitrary")),
        name="ada_mod",
    )(cond, ada_w, ada_b.reshape(depth, 1, n6))


AB_SEGMENT_EPILOGUES = ("gelu", "gelu_norm", "norm", "norm", "none")


def _ab_in_kernel(x_ref, gam_ref, sh_ref, sc_ref, w_ref, gain_ref, o_ref, h_ref, *, tn):
    h_ref[...] = _norm_mod(x_ref[0], gam_ref[...], sh_ref[0], sc_ref[0]).astype(BF16)
    per_seg = A_WIDTH // tn
    for jt in range(w_ref.shape[1] // tn):
        kind = AB_SEGMENT_EPILOGUES[jt // per_seg]
        acc = jnp.dot(h_ref[...], w_ref[:, jt * tn:(jt + 1) * tn], preferred_element_type=F32)
        if kind.startswith("gelu"):
            acc = _gelu_tanh(acc)
        if kind.endswith("norm"):
            for g in range(tn // LANES):
                sl = slice(jt * tn + g * LANES, jt * tn + (g + 1) * LANES)
                ag = acc[:, g * LANES:(g + 1) * LANES]
                ms = jnp.mean(ag * ag, axis=-1, keepdims=True)
                o_ref[0, :, sl] = (ag * lax.rsqrt(ms + EPS) * gain_ref[:, sl]).astype(o_ref.dtype)
        else:
            o_ref[0, :, jt * tn:(jt + 1) * tn] = acc.astype(o_ref.dtype)


def _resident(shape):
    nd = len(shape)
    return pl.BlockSpec(shape, lambda *_: (0,) * nd, pipeline_mode=pl.Buffered(1))


def _ab_in(x, gamma, shift, scale, w, gain, tm, tn=512):
    b, n, d = x.shape
    f = w.shape[1]
    return pl.pallas_call(
        functools.partial(_ab_in_kernel, tn=tn),
        grid=(b, n // tm),
        in_specs=[
            pl.BlockSpec((1, tm, d), lambda bi, i: (bi, i, 0)),
            _resident((1, d)),
            pl.BlockSpec((1, 1, d), lambda bi, i: (bi, 0, 0)),
            pl.BlockSpec((1, 1, d), lambda bi, i: (bi, 0, 0)),
            _resident((d, f)),
            _resident((1, f)),
        ],
        out_specs=pl.BlockSpec((1, tm, f), lambda bi, i: (bi, i, 0)),
        out_shape=jax.ShapeDtypeStruct((b, n, f), BF16),
        scratch_shapes=[pltpu.VMEM((tm, d), BF16)],
        compiler_params=_params(("arbitrary", "arbitrary")),
        name="ab_in_proj",
    )(x, gamma, shift, scale, w, gain)


def _gmlp_kernel(u_ref, v_ref, ws_ref, bs_ref, o_ref):
    tm = u_ref.shape[1]
    for ch in range(tm // CHUNK):
        rows = slice(ch * CHUNK, (ch + 1) * CHUNK)
        for g in range(A_GROUPS):
            cols = slice(g * A_DIM, (g + 1) * A_DIM)
            s = jnp.dot(ws_ref[g], v_ref[0, rows, cols], preferred_element_type=F32) + bs_ref[g]
            o_ref[0, rows, cols] = (u_ref[0, rows, cols].astype(F32) * s).astype(o_ref.dtype)


def _gmlp(p, ws, bs_b, tm):
    b, n, _ = p.shape
    return pl.pallas_call(
        _gmlp_kernel,
        grid=(b, n // tm),
        in_specs=[
            pl.BlockSpec((1, tm, A_WIDTH), lambda bi, i: (bi, i, 0)),
            pl.BlockSpec((1, tm, A_WIDTH), lambda bi, i: (bi, i, 1)),
            pl.BlockSpec((A_GROUPS, CHUNK, CHUNK), lambda bi, i: (0, 0, 0)),
            pl.BlockSpec((A_GROUPS, CHUNK, A_DIM), lambda bi, i: (0, 0, 0)),
        ],
        out_specs=pl.BlockSpec((1, tm, A_WIDTH), lambda bi, i: (bi, i, 0)),
        out_shape=jax.ShapeDtypeStruct((b, n, A_WIDTH), BF16),
        compiler_params=_params(("arbitrary", "arbitrary")),
        name="gmlp",
    )(p, p, ws, bs_b)


Q_ROWS = 4
KV_BLOCKS = 3
MASKED_PLANE = 2 * NA_ROWS - 1


def _na_bias_table(rpb, rows):
    h = rpb.shape[0]
    cols = np.arange(GRID_W)
    cstart = np.clip(cols - NA_COLS // 2, 0, GRID_W - NA_COLS)
    valid = (cols[None, :] >= cstart[:, None]) & (cols[None, :] < cstart[:, None] + NA_COLS)
    col_off = np.clip(cols[None, :] - cols[:, None] + (NA_COLS - 1), 0, 2 * NA_COLS - 2)
    planes = jnp.where(valid[None, None], rpb[:, :, col_off], MASK_VALUE)
    planes = jnp.concatenate([planes, jnp.full((h, 1, GRID_W, GRID_W), MASK_VALUE, planes.dtype)], axis=1)
    nblk = rows // Q_ROWS
    assert nblk >= KV_BLOCKS + 1 and NA_ROWS <= (KV_BLOCKS - 1) * Q_ROWS
    sel = np.full((3, Q_ROWS, KV_BLOCKS * Q_ROWS), MASKED_PLANE, np.int32)
    for variant, g in enumerate((0, 1, nblk - 1)):
        first = min(max(g - 1, 0), nblk - KV_BLOCKS)
        for i in range(Q_ROWS):
            r = g * Q_ROWS + i
            rs = min(max(r - NA_ROWS // 2, 0), rows - NA_ROWS)
            for kr in range(KV_BLOCKS * Q_ROWS):
                key_row = first * Q_ROWS + kr
                if rs <= key_row < rs + NA_ROWS:
                    sel[variant, i, kr] = key_row - r + (NA_ROWS - 1)
    t = planes[:, sel]
    t = jnp.transpose(t, (1, 0, 2, 4, 3, 5))
    return t.reshape(3, h, Q_ROWS * GRID_W, KV_BLOCKS * Q_ROWS * GRID_W).astype(F32)


def _na_kernel(q_ref, k0_ref, k1_ref, k2_ref, v0_ref, v1_ref, v2_ref, kx_ref, vx_ref, bias_ref, o_ref):
    dn = (((1,), (1,)), ((), ()))
    tk = k0_ref.shape[1]
    k_refs = (k0_ref, k1_ref, k2_ref)
    v_refs = (v0_ref, v1_ref, v2_ref)
    for h in range(B_HEADS):
        cols = slice(h * B_DIM, (h + 1) * B_DIM)
        q = q_ref[0, :, cols]
        s = [lax.dot_general(q, k_refs[n][0, :, cols], dn, preferred_element_type=F32)
             + bias_ref[0, h, :, n * tk:(n + 1) * tk] for n in range(KV_BLOCKS)]
        s.append(lax.dot_general(q, kx_ref[0, :, cols], dn, preferred_element_type=F32))
        m = functools.reduce(jnp.maximum, [jnp.max(x, axis=-1, keepdims=True) for x in s])
        p = [jnp.exp(x - m) for x in s]
        l = functools.reduce(jnp.add, [jnp.sum(x, axis=-1, keepdims=True) for x in p])
        o = jnp.dot(p[KV_BLOCKS].astype(BF16), vx_ref[0, :, cols], preferred_element_type=F32)
        for n in range(KV_BLOCKS):
            o += jnp.dot(p[n].astype(BF16), v_refs[n][0, :, cols], preferred_element_type=F32)
        o_ref[0, :, cols] = (o / l).astype(o_ref.dtype)


def _neighborhood_attention(p, pc, bias):
    b, s, _ = p.shape
    lc = pc.shape[1]
    tq = Q_ROWS * GRID_W
    nblk = s // tq
    qcol, kcol, vcol = 2, 3, 4
    first = lambda i: jnp.clip(i - 1, 0, nblk - KV_BLOCKS)
    kv = lambda col, n: pl.BlockSpec((1, tq, B_WIDTH), lambda bi, i: (bi, first(i) + n, col))
    variant = lambda i: jnp.where(i == 0, 0, jnp.where(i == nblk - 1, 2, 1))
    return pl.pallas_call(
        _na_kernel,
        grid=(b, nblk),
        in_specs=[
            pl.BlockSpec((1, tq, B_WIDTH), lambda bi, i: (bi, i, qcol)),
            *[kv(kcol, n) for n in range(KV_BLOCKS)],
            *[kv(vcol, n) for n in range(KV_BLOCKS)],
            pl.BlockSpec((1, lc, B_WIDTH), lambda bi, i: (bi, 0, kcol)),
            pl.BlockSpec((1, lc, B_WIDTH), lambda bi, i: (bi, 0, vcol)),
            pl.BlockSpec((1, *bias.shape[1:]), lambda bi, i: (variant(i), 0, 0, 0)),
        ],
        out_specs=pl.BlockSpec((1, tq, B_WIDTH), lambda bi, i: (bi, i, 0)),
        out_shape=jax.ShapeDtypeStruct((b, s, B_WIDTH), BF16),
        compiler_params=_params(("arbitrary", "arbitrary")),
        name="neighborhood_attention",
    )(p, *([p] * (2 * KV_BLOCKS)), pc, pc, bias)


def _ctx_attn_kernel(q_ref, k_ref, v_ref, o_ref):
    for h in range(B_HEADS):
        cols = slice(h * B_DIM, (h + 1) * B_DIM)
        s = lax.dot_general(q_ref[0, :, cols], k_ref[0, :, cols], (((1,), (1,)), ((), ())),
                            preferred_element_type=F32)
        p = jnp.exp(s - jnp.max(s, axis=-1, keepdims=True))
        o = jnp.dot(p.astype(BF16), v_ref[0, :, cols], preferred_element_type=F32)
        o_ref[0, :, cols] = (o / jnp.sum(p, axis=-1, keepdims=True)).astype(o_ref.dtype)


def _context_attention(pc):
    b, lc, _ = pc.shape
    spec = lambda col: pl.BlockSpec((1, lc, B_WIDTH), lambda bi: (bi, 0, col))
    return pl.pallas_call(
        _ctx_attn_kernel,
        grid=(b,),
        in_specs=[spec(2), spec(3), spec(4)],
        out_specs=pl.BlockSpec((1, lc, B_WIDTH), lambda bi: (bi, 0, 0)),
        out_shape=jax.ShapeDtypeStruct((b, lc, B_WIDTH), BF16),
        compiler_params=_params(("arbitrary",)),
        name="context_attention",
    )(pc, pc, pc)


def _store_residual_and_route(xf_ref, gam_ref, sh_ref, sc_ref, wr_ref, o_ref, h_ref, aff_ref):
    tm = xf_ref.shape[0]
    xn = xf_ref[...]
    h = _norm_mod(xn, gam_ref[...], sh_ref[0], sc_ref[0])
    _slab_zero_pad(o_ref, (0,), tm)
    _slab_zero_pad(h_ref, (0,), tm)
    for c in range(SLAB_ROWS):
        sl = slice(c * LANES, (c + 1) * LANES)
        o_ref[_slab_cols(o_ref, (0,), c, tm)] = xn[:, sl]
        h_ref[_slab_cols(h_ref, (0,), c, tm)] = h[:, sl]
    ne = aff_ref.shape[1]
    h_hi = h.astype(BF16)
    h_lo = (h - h_hi.astype(F32)).astype(BF16)
    p_hi = jnp.dot(h_hi, wr_ref[...], preferred_element_type=F32).T
    p_lo = jnp.dot(h_lo, wr_ref[...], preferred_element_type=F32).T
    logits = p_hi[0:ne] + (p_hi[ne:2 * ne] + p_lo[0:ne])
    m = jnp.max(logits, axis=0, keepdims=True)
    e = jnp.exp(logits - m)
    aff_ref[0] = e / jnp.sum(e, axis=0, keepdims=True)


def _router_weight(router_w):
    d, ne = router_w.shape
    w_hi = router_w.astype(BF16)
    w_lo = (router_w - w_hi.astype(F32)).astype(BF16)
    return jnp.concatenate([w_hi, w_lo, jnp.zeros((d, LANES - 2 * ne), BF16)], axis=1)


def _route_specs(b, n, d, ne, tm):
    slab = pl.BlockSpec((1, tm * PITCH, LANES), lambda bi, i: (bi, i, 0))
    mod = pl.BlockSpec((1, 1, d), lambda bi, i: (bi, 0, 0))
    in_specs = [_resident((1, d)), mod, mod, _resident((d, LANES))]
    out_specs = [slab, slab, pl.BlockSpec((1, ne, tm), lambda bi, i: (bi, 0, i))]
    slab_shape = jax.ShapeDtypeStruct((b, n * PITCH, LANES), F32)
    return in_specs, out_specs, [slab_shape, slab_shape, jax.ShapeDtypeStruct((b, ne, n), F32)]


def _ab_out_kernel(a_ref, b_ref, w_ref, x_ref, g_ref, gam_ref, sh_ref, sc_ref, wr_ref, o_ref, h_ref, aff_ref,
                   xf_ref, *, tn):
    ka = a_ref.shape[2]
    for jt in range(w_ref.shape[1] // tn):
        cols = slice(jt * tn, (jt + 1) * tn)
        acc = jnp.dot(a_ref[0], w_ref[0:ka, cols], preferred_element_type=F32)
        acc += jnp.dot(b_ref[0], w_ref[ka:2 * ka, cols], preferred_element_type=F32)
        xf_ref[:, cols] = x_ref[0, :, cols] + g_ref[0, :, cols] * acc
    _store_residual_and_route(xf_ref, gam_ref, sh_ref, sc_ref, wr_ref, o_ref, h_ref, aff_ref)


def _ab_out(a, bm, w, x, gate, gamma2, shift2, scale2, wr_split, tm, tn=512):
    b, n, d = x.shape
    ka = a.shape[2]
    rin, rout, rshape = _route_specs(b, n, d, N_EXPERTS, tm)
    return pl.pallas_call(
        functools.partial(_ab_out_kernel, tn=tn),
        grid=(b, n // tm),
        in_specs=[
            pl.BlockSpec((1, tm, ka), lambda bi, i: (bi, i, 0)),
            pl.BlockSpec((1, tm, ka), lambda bi, i: (bi, i, 0)),
            _resident(w.shape),
            pl.BlockSpec((1, tm, d), lambda bi, i: (bi, i, 0)),
            pl.BlockSpec((1, 1, d), lambda bi, i: (bi, 0, 0)),
            *rin,
        ],
        out_specs=rout,
        out_shape=rshape,
        scratch_shapes=[pltpu.VMEM((tm, d), F32)],
        compiler_params=_params(("arbitrary", "arbitrary")),
        name="ab_out_proj",
    )(a, bm, w, x, gate, gamma2, shift2, scale2, wr_split)


def _c_in_kernel(x_ref, gam_ref, sh_ref, sc_ref, w_ref, bg_ref, cz_ref, h_ref, xf_ref, *, tn):
    _slab_load_rows(x_ref, (0,), h_ref.shape[0], xf_ref)
    h_ref[...] = _norm_mod(xf_ref[...], gam_ref[...], sh_ref[0], sc_ref[0]).astype(BF16)
    cw = w_ref.shape[1] // 3
    for jt in range(cw // tn):
        cols = slice(jt * tn, (jt + 1) * tn)
        proj = lambda seg: jnp.dot(h_ref[...], w_ref[:, seg * cw + jt * tn:seg * cw + (jt + 1) * tn],
                                   preferred_element_type=F32)
        bg_ref[0, :, cols] = proj(0).astype(bg_ref.dtype)
        cz_ref[0, :, cols] = (proj(1) * proj(2)).astype(cz_ref.dtype)


def _c_in(x, gamma, shift, scale, w, tm, tn=512):
    b = x.shape[0]
    n = x.shape[1] // PITCH
    d = w.shape[0]
    cw = w.shape[1] // 3
    out = jax.ShapeDtypeStruct((b, n, cw), BF16)
    ospec = pl.BlockSpec((1, tm, cw), lambda bi, i: (bi, i, 0))
    return pl.pallas_call(
        functools.partial(_c_in_kernel, tn=tn),
        grid=(b, n // tm),
        in_specs=[
            pl.BlockSpec((1, tm * PITCH, LANES), lambda bi, i: (bi, i, 0)),
            _resident((1, d)),
            pl.BlockSpec((1, 1, d), lambda bi, i: (bi, 0, 0)),
            pl.BlockSpec((1, 1, d), lambda bi, i: (bi, 0, 0)),
            _resident(w.shape),
        ],
        out_specs=[ospec, ospec],
        out_shape=[out, out],
        scratch_shapes=[pltpu.VMEM((tm, d), BF16), pltpu.VMEM((tm, d), F32)],
        compiler_params=_params(("arbitrary", "arbitrary")),
        name="c_in_proj",
    )(x, gamma, shift, scale, w)


HALO = 8


def _c_out_kernel(bg_ref, cz_ref, czp_ref, czn_ref, cw_ref, w_ref, x_ref, g_ref, gam_ref, sh_ref, sc_ref, wr_ref,
                  o_ref, h_ref, aff_ref, xf_ref, acc_ref, *, kchunk, tn):
    i = pl.program_id(1)
    ni = pl.num_programs(1)
    tm = cz_ref.shape[1]
    cw = cz_ref.shape[2]
    _slab_load_rows(x_ref, (0,), tm, xf_ref)
    row = lax.broadcasted_iota(jnp.int32, (tm, kchunk), 0)
    for kc in range(cw // kchunk):
        ks = slice(kc * kchunk, (kc + 1) * kchunk)
        cz = cz_ref[0, :, ks].astype(F32)
        prev_row = jnp.where(i > 0, czp_ref[0, HALO - 1:HALO, ks].astype(F32), 0.0)
        next_row = jnp.where(i < ni - 1, czn_ref[0, 0:1, ks].astype(F32), 0.0)
        up = jnp.where(row == 0, prev_row, pltpu.roll(cz, 1, 0))
        dn = jnp.where(row == tm - 1, next_row, pltpu.roll(cz, tm - 1, 0))
        y = cw_ref[0:1, ks] * up + cw_ref[1:2, ks] * cz + cw_ref[2:3, ks] * dn
        lhs = (bg_ref[0, :, ks].astype(F32) * y).astype(BF16)
        for jt in range(w_ref.shape[1] // tn):
            cols = slice(jt * tn, (jt + 1) * tn)
            part = jnp.dot(lhs, w_ref[ks, cols], preferred_element_type=F32)
            if kc == 0:
                acc_ref[:, cols] = part
            else:
                acc_ref[:, cols] += part
    xf_ref[...] = xf_ref[...] + g_ref[0] * acc_ref[...]
    _store_residual_and_route(xf_ref, gam_ref, sh_ref, sc_ref, wr_ref, o_ref, h_ref, aff_ref)


def _c_out(bg, cz, conv_w, w, x, gate, gamma2, shift2, scale2, wr_split, tm, tn=512):
    b, n, cw = bg.shape
    d = w.shape[1]
    hb = tm // HALO
    nh = n // HALO
    rin, rout, rshape = _route_specs(b, n, d, N_EXPERTS, tm)
    return pl.pallas_call(
        functools.partial(_c_out_kernel, kchunk=512, tn=tn),
        grid=(b, n // tm),
        in_specs=[
            pl.BlockSpec((1, tm, cw), lambda bi, i: (bi, i, 0)),
            pl.BlockSpec((1, tm, cw), lambda bi, i: (bi, i, 0)),
            pl.BlockSpec((1, HALO, cw), lambda bi, i: (bi, jnp.maximum(i * hb - 1, 0), 0)),
            pl.BlockSpec((1, HALO, cw), lambda bi, i: (bi, jnp.minimum((i + 1) * hb, nh - 1), 0)),
            _resident((CONV_W, cw)),
            _resident(w.shape),
            pl.BlockSpec((1, tm * PITCH, LANES), lambda bi, i: (bi, i, 0)),
            pl.BlockSpec((1, 1, d), lambda bi, i: (bi, 0, 0)),
            *rin,
        ],
        out_specs=rout,
        out_shape=rshape,
        scratch_shapes=[pltpu.VMEM((tm, d), F32), pltpu.VMEM((tm, d), F32)],
        compiler_params=_params(("arbitrary", "arbitrary")),
        name="c_out_proj",
    )(bg, cz, cz, cz, conv_w, w, x, gate, gamma2, shift2, scale2, wr_split)


SEARCH_BITS = 3


def _select_kernel(aff_ref, idx_ref, gate_ref, *, cap, cchunk):
    v = aff_ref[0, 0]
    nr = v.shape[0]

    def count(mask):
        return jnp.sum(jnp.sum(mask.astype(F32), axis=1, keepdims=True), axis=0, keepdims=True)

    def search(it, t):
        shift = 30 - SEARCH_BITS * (it + 1)
        digit = jnp.zeros((1, 1), jnp.int32)
        for j in range(1, 2 ** SEARCH_BITS):
            cand = pltpu.bitcast(t | (jnp.int32(j) << shift), F32)
            digit += (count(v >= cand) >= cap).astype(jnp.int32)
        return t | (digit << shift)

    thr = pltpu.bitcast(lax.fori_loop(0, 30 // SEARCH_BITS, search, jnp.zeros((1, 1), jnp.int32)), F32)
    gt = v > thr
    eq = v == thr
    need = cap - count(gt)

    lane_l = lax.broadcasted_iota(jnp.int32, (LANES, LANES), 0)
    lane_c = lax.broadcasted_iota(jnp.int32, (LANES, LANES), 1)
    tri_lane = (lane_l <= lane_c).astype(BF16)
    row_r = lax.broadcasted_iota(jnp.int32, (nr, nr), 0)
    row_c = lax.broadcasted_iota(jnp.int32, (nr, nr), 1)
    tri_row = (row_c <= row_r).astype(BF16)

    def prefix(mask):
        mf = mask.astype(BF16)
        in_row = jnp.dot(mf, tri_lane, preferred_element_type=F32)
        colcum = jnp.dot(tri_row, mf, preferred_element_type=F32)
        row_incl = jnp.sum(colcum, axis=1, keepdims=True)
        row_tot = jnp.sum(mask.astype(F32), axis=1, keepdims=True)
        return in_row, row_incl - row_tot, row_incl

    eq_in, eq_off, _ = prefix(eq)
    eq_rank = eq_in + eq_off - eq.astype(F32)
    sel = gt | (eq & (eq_rank < need))
    _, sel_off, sel_incl = prefix(sel)

    self_bf = sel.astype(BF16)
    tri_lane_t = (lane_c <= lane_l).astype(BF16)
    dn_t = (((1,), (1,)), ((), ()))
    pt = lax.dot_general(tri_lane_t, self_bf, dn_t, preferred_element_type=F32)
    eye = (lane_l == lane_c).astype(F32)
    vt = lax.dot_general(eye, v, dn_t, precision=HIGHEST, preferred_element_type=F32)

    for c0 in range(0, cap, cchunk):
        cc = min(cchunk, cap - c0)
        slot = (lax.broadcasted_iota(jnp.int32, (1, cc), 1) + c0).astype(F32)
        r_of = jnp.sum((sel_incl <= slot).astype(F32), axis=0, keepdims=True)
        onehot = (lax.broadcasted_iota(jnp.int32, (nr, cc), 0).astype(F32) == r_of)
        onehot_f = onehot.astype(F32)
        local = slot - jnp.sum(onehot_f * sel_off, axis=0, keepdims=True)
        prow = jnp.dot(pt.astype(BF16), onehot.astype(BF16), preferred_element_type=F32)
        l_of = jnp.sum((prow <= local).astype(F32), axis=0, keepdims=True)
        vrow = jnp.dot(vt, onehot_f, precision=HIGHEST, preferred_element_type=F32)
        lane_i = lax.broadcasted_iota(jnp.int32, (LANES, cc), 0).astype(F32)
        gsel = jnp.sum(jnp.where(lane_i == l_of, vrow, 0.0), axis=0, keepdims=True)
        idx_ref[0, 0, :, c0:c0 + cc] = (r_of * LANES + l_of).astype(jnp.int32)
        gate_ref[0, 0, :, c0:c0 + cc] = gsel


MIN_SELECT_ROWS = 8


def _select(aff_t, cap):
    b, ne, n = aff_t.shape
    if n < MIN_SELECT_ROWS * LANES:
        assert cap <= n
        aff_t = jnp.pad(aff_t, ((0, 0), (0, 0), (0, MIN_SELECT_ROWS * LANES - n)), constant_values=-1.0)
        n = MIN_SELECT_ROWS * LANES
    nr = n // LANES
    out = lambda dt: jax.ShapeDtypeStruct((b, ne, 1, cap), dt)
    ospec = pl.BlockSpec((1, 1, 1, cap), lambda bi, e: (bi, e, 0, 0))
    idx, gate = pl.pallas_call(
        functools.partial(_select_kernel, cap=cap, cchunk=512),
        grid=(b, ne),
        in_specs=[pl.BlockSpec((1, 1, nr, LANES), lambda bi, e: (bi, e, 0, 0))],
        out_specs=[ospec, ospec],
        out_shape=[out(jnp.int32), out(F32)],
        compiler_params=_params(("arbitrary", "arbitrary")),
        name="moe_select",
    )(aff_t.reshape(b, ne, nr, LANES))
    return idx.reshape(b, ne, cap), gate.reshape(b, ne, cap)


X_SLOTS = 2
O_SLOTS = 3
ISSUE_UNROLL = 8


def _moe_kernel(idx_ref, idxn_ref, g_ref, g2_ref, h_hbm, wg_ref, wu_ref, wd_ref, x_hbm, o_hbm,
                xg, og, xs_ref, sem_x, sem_o, sem_s, *, n_tok):
    del x_hbm
    e, bi, t = pl.program_id(0), pl.program_id(1), pl.program_id(2)
    nb, nt = pl.num_programs(1), pl.num_programs(2)
    step = (e * nb + bi) * nt + t
    last = pl.num_programs(0) * nb * nt - 1
    tc = xs_ref.shape[0]
    moved = tc * SLAB_ROWS

    def for_each_slot_row(idx, sample, fn):
        def body(s8, carry):
            for u in range(ISSUE_UNROLL):
                s = s8 * ISSUE_UNROLL + u
                src = pl.multiple_of((sample * n_tok + idx[0, 0, s]) * PITCH, 4)
                fn(pl.ds(src, SLAB_ROWS), pl.ds(pl.multiple_of(s * PITCH, 4), SLAB_ROWS))
            return carry
        lax.fori_loop(0, tc // ISSUE_UNROLL, body, 0)

    def issue_gathers(idx, sample, xslot, oslot):
        def one(src, dst):
            pltpu.make_async_copy(h_hbm.at[src, :], xg.at[xslot, dst, :], sem_x.at[xslot]).start()
            pltpu.make_async_copy(o_hbm.at[src, :], og.at[oslot, dst, :], sem_o.at[oslot]).start()
        for_each_slot_row(idx, sample, one)

    def wait_rows(hbm, buf, slot, sem, to_hbm):
        a, b = hbm.at[pl.ds(0, moved), :], buf.at[slot, pl.ds(0, moved), :]
        (pltpu.make_async_copy(b, a, sem.at[slot]) if to_hbm else pltpu.make_async_copy(a, b, sem.at[slot])).wait()

    @pl.when(step == 0)
    def _():
        issue_gathers(idx_ref, bi, 0, 0)

    @pl.when(step >= 2)
    def _():
        wait_rows(o_hbm, og, (step - 2) % O_SLOTS, sem_s, True)

    @pl.when(step < last)
    def _():
        nxt = step + 1
        issue_gathers(idxn_ref, (nxt // nt) % nb, nxt % X_SLOTS, nxt % O_SLOTS)

    xslot = step % X_SLOTS
    oslot = step % O_SLOTS
    wait_rows(h_hbm, xg, xslot, sem_x, False)
    for c in range(SLAB_ROWS):
        xs_ref[:, c * LANES:(c + 1) * LANES] = xg[xslot, pl.ds(c, tc, stride=PITCH), :].astype(BF16)
    xs = xs_ref[...]
    gate = jnp.dot(xs, wg_ref[0, 0], preferred_element_type=F32)
    up = jnp.dot(xs, wu_ref[0, 0], preferred_element_type=F32)
    hid = (_silu(gate) * up).astype(BF16)
    y = jnp.dot(hid, wd_ref[0, 0], preferred_element_type=F32)
    eye = lax.broadcasted_iota(jnp.int32, (tc, tc), 0) == lax.broadcasted_iota(jnp.int32, (tc, tc), 1)
    gcol = jnp.sum(jnp.where(eye, g_ref[0], 0.0), axis=1, keepdims=True)
    y = y * gcol

    wait_rows(o_hbm, og, oslot, sem_o, False)
    for c in range(SLAB_ROWS):
        cols = slice(c * LANES, (c + 1) * LANES)
        rows = (oslot, pl.ds(c, tc, stride=PITCH), slice(None))
        og[rows] = og[rows] + g2_ref[0][:, cols] * y[:, cols]

    def scatter(src, dst):
        pltpu.make_async_copy(og.at[oslot, dst, :], o_hbm.at[src, :], sem_s.at[oslot]).start()
    for_each_slot_row(idx_ref, bi, scatter)

    @pl.when(step == last)
    def _():
        @pl.when(step >= 1)
        def _():
            wait_rows(o_hbm, og, (step - 1) % O_SLOTS, sem_s, True)
        wait_rows(o_hbm, og, oslot, sem_s, True)


def _moe_experts(idx, gate, h, wg, wu, wd, layer, gate2, x, tc):
    b, ne, cap = idx.shape
    n = x.shape[1] // PITCH
    d, f = wg.shape[2], wg.shape[3]
    nt = cap // tc
    assert b >= 2 and nt >= 2 and tc % ISSUE_UNROLL == 0 and d == SLAB_ROWS * LANES
    nsteps = ne * b * nt

    def cur(e, bi, t):
        return ((bi * ne + e) * nt + t, 0, 0)

    def nxt(e, bi, t):
        step = jnp.minimum((e * b + bi) * nt + t + 1, nsteps - 1)
        return (((step // nt) % b * ne + step // (nt * b)) * nt + step % nt, 0, 0)

    idx3 = idx.reshape(b * ne * nt, 1, tc)
    rows = (tc * PITCH, LANES)
    out = pl.pallas_call(
        functools.partial(_moe_kernel, n_tok=n),
        grid=(ne, b, nt),
        in_specs=[
            pl.BlockSpec((1, 1, tc), cur, memory_space=pltpu.SMEM),
            pl.BlockSpec((1, 1, tc), nxt, memory_space=pltpu.SMEM),
            pl.BlockSpec((1, 1, tc), cur),
            pl.BlockSpec((1, 1, d), lambda e, bi, t: (bi, 0, 0)),
            pl.BlockSpec(memory_space=pl.ANY),
            pl.BlockSpec((1, 1, d, f), lambda e, bi, t: (layer, e, 0, 0)),
            pl.BlockSpec((1, 1, d, f), lambda e, bi, t: (layer, e, 0, 0)),
            pl.BlockSpec((1, 1, f, d), lambda e, bi, t: (layer, e, 0, 0)),
            pl.BlockSpec(memory_space=pl.ANY),
        ],
        out_specs=pl.BlockSpec(memory_space=pl.ANY),
        out_shape=jax.ShapeDtypeStruct((b * n * PITCH, LANES), F32),
        input_output_aliases={8: 0},
        scratch_shapes=[
            pltpu.VMEM((X_SLOTS, *rows), F32), pltpu.VMEM((O_SLOTS, *rows), F32), pltpu.VMEM((tc, d), BF16),
            pltpu.SemaphoreType.DMA((X_SLOTS,)), pltpu.SemaphoreType.DMA((O_SLOTS,)),
            pltpu.SemaphoreType.DMA((O_SLOTS,)),
        ],
        compiler_params=_params(("arbitrary", "arbitrary", "arbitrary"), disable_bounds_checks=True),
        name="moe_experts",
    )(idx3, idx3, gate.reshape(b * ne * nt, 1, tc), gate2, h.reshape(b * n * PITCH, LANES), wg, wu, wd,
      x.reshape(b * n * PITCH, LANES))
    return out.reshape(x.shape)


def _ec_moe_residual(x, h, aff_t, gate2, wg, wu, wd, layer, *, tc):
    n = x.shape[1] // PITCH
    cap = CAP_FACTOR * n // N_EXPERTS
    idx, g = _select(aff_t, cap)
    return _moe_experts(idx, g, h, wg, wu, wd, layer, gate2, x, tc)


def _slab_to_std_kernel(x_ref, o_ref):
    _slab_load_rows(x_ref, (0,), o_ref.shape[1], o_ref.at[0])


def _slab_to_std(x, d, tm):
    b = x.shape[0]
    n = x.shape[1] // PITCH
    return pl.pallas_call(
        _slab_to_std_kernel,
        grid=(b, n // tm),
        in_specs=[pl.BlockSpec((1, tm * PITCH, LANES), lambda bi, i: (bi, i, 0))],
        out_specs=pl.BlockSpec((1, tm, d), lambda bi, i: (bi, i, 0)),
        out_shape=jax.ShapeDtypeStruct((b, n, d), F32),
        compiler_params=_params(("arbitrary", "arbitrary")),
        name="slab_to_std",
    )(x)


CTX_MOE_TILE = 16


def kernel(x, c, ctx, c_ctx, ada_w, ada_b, norm1_g, norm2_g, ab_w_in, ab_w_out, a_ws, a_bs, a_vnorm_g,
           b_qnorm_g, b_knorm_g, b_rpb, c_w_in, c_conv_w, c_w_out, router_w, moe_w_gate, moe_w_up, moe_w_down):
    bsz, seq, d = x.shape
    lc = ctx.shape[1]
    depth = ada_w.shape[0]
    assert depth == 2, "layer plan: mixer A/B layer (updates the context stream) then mixer C layer"

    cond = jnp.concatenate([c, c_ctx[None], jnp.zeros((8 - bsz - 1, d), F32)], axis=0)
    mod = _ada(cond, ada_w, ada_b)

    wg_all, wu_all, wd_all = moe_w_gate.astype(BF16), moe_w_up.astype(BF16), moe_w_down.astype(BF16)
    xl = x
    xc = ctx
    for i in range(depth):
        upd_ctx = i < depth - 1
        sh1, sc1, g1, sh2, sc2, g2 = [mod[i, :bsz, k * d:(k + 1) * d].reshape(bsz, 1, d) for k in range(6)]
        csh1, csc1, cg1, csh2, csc2, cg2 = [
            jnp.broadcast_to(mod[i, bsz, k * d:(k + 1) * d].reshape(1, 1, d), (bsz, 1, d)) for k in range(6)]
        gam1 = norm1_g[i].reshape(1, d)
        gam2 = norm2_g[i].reshape(1, d)
        wr_split = _router_weight(router_w[i])
        j = i // 2
        if i % 2 == 0:
            w_in = ab_w_in[j].astype(BF16)
            w_out = ab_w_out[j].astype(BF16)
            ones = jnp.ones((A_WIDTH,), F32)
            gain = jnp.concatenate([
                ones, a_vnorm_g[j].reshape(-1),
                jnp.tile(b_qnorm_g[j], B_HEADS) * (B_DIM ** -0.5),
                jnp.tile(b_knorm_g[j], B_HEADS), ones]).reshape(1, -1)
            p = _ab_in(xl, gam1, sh1, sc1, w_in, gain, tm=512)
            pc = _ab_in(xc, gam1, csh1, csc1, w_in, gain, tm=lc)
            ws = a_ws[j].astype(BF16)
            bs_b = jnp.broadcast_to(a_bs[j][:, :, None], (A_GROUPS, CHUNK, A_DIM)).astype(F32)
            a_l = _gmlp(p, ws, bs_b, tm=512)
            b_l = _neighborhood_attention(p, pc, _na_bias_table(b_rpb[j], seq // GRID_W))
            xl, h2, aff_t = _ab_out(a_l, b_l, w_out, xl, g1, gam2, sh2, sc2, wr_split, tm=512)
            if upd_ctx:
                a_c = _gmlp(pc, ws, bs_b, tm=lc)
                b_c = _context_attention(pc)
                xc, hc2, affc_t = _ab_out(a_c, b_c, w_out, xc, cg1, gam2, csh2, csc2, wr_split, tm=lc)
        else:
            assert not upd_ctx
            bg, cz = _c_in(xl, gam1, sh1, sc1, c_w_in[j].astype(BF16), tm=512)
            xl, h2, aff_t = _c_out(bg, cz, c_conv_w[j], c_w_out[j].astype(BF16), xl, g1, gam2, sh2, sc2,
                                   wr_split, tm=512)

        xl = _ec_moe_residual(xl, h2, aff_t, g2, wg_all, wu_all, wd_all, i, tc=256)
        if upd_ctx:
            xc = _ec_moe_residual(xc, hc2, affc_t, cg2, wg_all, wu_all, wd_all, i, tc=CTX_MOE_TILE)
            xc = _slab_to_std(xc, d, tm=lc)
    return _slab_to_std(xl, d, tm=512)
```

```python
import functools
import math

import jax
import jax.numpy as jnp
import numpy as np
from jax import lax
from jax.experimental import pallas as pl
from jax.experimental.pallas import tpu as pltpu

GRID_W = 64
CHUNK = 128
A_GROUPS = 8
A_DIM = 128
A_WIDTH = A_GROUPS * A_DIM
B_HEADS = 8
B_DIM = 128
B_WIDTH = B_HEADS * B_DIM
NA_ROWS = 8
NA_COLS = 16
CONV_W = 3
N_EXPERTS = 16
CAP_FACTOR = 2
EPS = 1e-6

LANES = 128
VMEM_LIMIT = 56 * 1024 * 1024

F32 = jnp.float32
BF16 = jnp.bfloat16
HIGHEST = lax.Precision.HIGHEST
MASK_VALUE = -1e30


SLAB_ROWS = 16
PITCH = 20


def _params(sem, **kw):
    return pltpu.CompilerParams(dimension_semantics=sem, vmem_limit_bytes=VMEM_LIMIT, **kw)


def _slab_cols(ref, lead, c, tm):
    return (*lead, pl.ds(c, tm, stride=PITCH), slice(None))


def _slab_zero_pad(ref, lead, tm):
    for c in range(SLAB_ROWS, PITCH):
        ref[_slab_cols(ref, lead, c, tm)] = jnp.zeros((tm, LANES), ref.dtype)


def _slab_load_rows(ref, lead, tm, dst_ref):
    for c in range(SLAB_ROWS):
        dst_ref[:, c * LANES:(c + 1) * LANES] = ref[_slab_cols(ref, lead, c, tm)]


def _silu(x):
    return x * (1.0 / (1.0 + jnp.exp(-x)))


def _gelu_tanh(x):
    return 0.5 * x * (1.0 + jnp.tanh(math.sqrt(2.0 / math.pi) * (x + 0.044715 * (x * x * x))))


def _norm_mod(x, gamma, shift, scale):
    ms = jnp.mean(x * x, axis=-1, keepdims=True)
    return (x * lax.rsqrt(ms + EPS) * gamma) * (1.0 + scale) + shift


def _ada_kernel(c_ref, w_ref, b_ref, o_ref):
    s = _silu(c_ref[...])
    o_ref[0] = jnp.dot(s, w_ref[0], precision=HIGHEST, preferred_element_type=F32) + b_ref[0]


def _ada(cond, ada_w, ada_b):
    depth, d, n6 = ada_w.shape
    tn = 1024
    return pl.pallas_call(
        _ada_kernel,
        grid=(depth, n6 // tn),
        in_specs=[
            pl.BlockSpec((8, d), lambda l, j: (0, 0)),
            pl.BlockSpec((1, d, tn), lambda l, j: (l, 0, j)),
            pl.BlockSpec((1, 1, tn), lambda l, j: (l, 0, j)),
        ],
        out_specs=pl.BlockSpec((1, 8, tn), lambda l, j: (l, 0, j)),
        out_shape=jax.ShapeDtypeStruct((depth, 8, n6), F32),
        compiler_params=_params(("arbitrary", "arbitrary")),
        name="ada_mod",
    )(cond, ada_w, ada_b.reshape(depth, 1, n6))


AB_SEGMENT_EPILOGUES = ("gelu", "gelu_norm", "norm", "norm", "none")


def _ab_in_kernel(x_ref, gam_ref, sh_ref, sc_ref, w_ref, gain_ref, o_ref, h_ref, *, tn):
    h_ref[...] = _norm_mod(x_ref[0], gam_ref[...], sh_ref[0], sc_ref[0]).astype(BF16)
    per_seg = A_WIDTH // tn
    for jt in range(w_ref.shape[1] // tn):
        kind = AB_SEGMENT_EPILOGUES[jt // per_seg]
        acc = jnp.dot(h_ref[...], w_ref[:, jt * tn:(jt + 1) * tn], preferred_element_type=F32)
        if kind.startswith("gelu"):
            acc = _gelu_tanh(acc)
        if kind.endswith("norm"):
            for g in range(tn // LANES):
                sl = slice(jt * tn + g * LANES, jt * tn + (g + 1) * LANES)
                ag = acc[:, g * LANES:(g + 1) * LANES]
                ms = jnp.mean(ag * ag, axis=-1, keepdims=True)
                o_ref[0, :, sl] = (ag * lax.rsqrt(ms + EPS) * gain_ref[:, sl]).astype(o_ref.dtype)
        else:
            o_ref[0, :, jt * tn:(jt + 1) * tn] = acc.astype(o_ref.dtype)


def _resident(shape):
    nd = len(shape)
    return pl.BlockSpec(shape, lambda *_: (0,) * nd, pipeline_mode=pl.Buffered(1))


def _ab_in(x, gamma, shift, scale, w, gain, tm, tn=512):
    b, n, d = x.shape
    f = w.shape[1]
    return pl.pallas_call(
        functools.partial(_ab_in_kernel, tn=tn),
        grid=(b, n // tm),
        in_specs=[
            pl.BlockSpec((1, tm, d), lambda bi, i: (bi, i, 0)),
            _resident((1, d)),
            pl.BlockSpec((1, 1, d), lambda bi, i: (bi, 0, 0)),
            pl.BlockSpec((1, 1, d), lambda bi, i: (bi, 0, 0)),
            _resident((d, f)),
            _resident((1, f)),
        ],
        out_specs=pl.BlockSpec((1, tm, f), lambda bi, i: (bi, i, 0)),
        out_shape=jax.ShapeDtypeStruct((b, n, f), BF16),
        scratch_shapes=[pltpu.VMEM((tm, d), BF16)],
        compiler_params=_params(("arbitrary", "arbitrary")),
        name="ab_in_proj",
    )(x, gamma, shift, scale, w, gain)


def _gmlp_kernel(u_ref, v_ref, ws_ref, bs_ref, o_ref):
    tm = u_ref.shape[1]
    for ch in range(tm // CHUNK):
        rows = slice(ch * CHUNK, (ch + 1) * CHUNK)
        for g in range(A_GROUPS):
            cols = slice(g * A_DIM, (g + 1) * A_DIM)
            s = jnp.dot(ws_ref[g], v_ref[0, rows, cols], preferred_element_type=F32) + bs_ref[g]
            o_ref[0, rows, cols] = (u_ref[0, rows, cols].astype(F32) * s).astype(o_ref.dtype)


def _gmlp(p, ws, bs_b, tm):
    b, n, _ = p.shape
    return pl.pallas_call(
        _gmlp_kernel,
        grid=(b, n // tm),
        in_specs=[
            pl.BlockSpec((1, tm, A_WIDTH), lambda bi, i: (bi, i, 0)),
            pl.BlockSpec((1, tm, A_WIDTH), lambda bi, i: (bi, i, 1)),
            pl.BlockSpec((A_GROUPS, CHUNK, CHUNK), lambda bi, i: (0, 0, 0)),
            pl.BlockSpec((A_GROUPS, CHUNK, A_DIM), lambda bi, i: (0, 0, 0)),
        ],
        out_specs=pl.BlockSpec((1, tm, A_WIDTH), lambda bi, i: (bi, i, 0)),
        out_shape=jax.ShapeDtypeStruct((b, n, A_WIDTH), BF16),
        compiler_params=_params(("arbitrary", "arbitrary")),
        name="gmlp",
    )(p, p, ws, bs_b)


Q_ROWS = 4
KV_BLOCKS = 3
MASKED_PLANE = 2 * NA_ROWS - 1


def _na_bias_table(rpb, rows):
    h = rpb.shape[0]
    cols = np.arange(GRID_W)
    cstart = np.clip(cols - NA_COLS // 2, 0, GRID_W - NA_COLS)
    valid = (cols[None, :] >= cstart[:, None]) & (cols[None, :] < cstart[:, None] + NA_COLS)
    col_off = np.clip(cols[None, :] - cols[:, None] + (NA_COLS - 1), 0, 2 * NA_COLS - 2)
    planes = jnp.where(valid[None, None], rpb[:, :, col_off], MASK_VALUE)
    planes = jnp.concatenate([planes, jnp.full((h, 1, GRID_W, GRID_W), MASK_VALUE, planes.dtype)], axis=1)
    nblk = rows // Q_ROWS
    assert nblk >= KV_BLOCKS + 1 and NA_ROWS <= (KV_BLOCKS - 1) * Q_ROWS
    sel = np.full((3, Q_ROWS, KV_BLOCKS * Q_ROWS), MASKED_PLANE, np.int32)
    for variant, g in enumerate((0, 1, nblk - 1)):
        first = min(max(g - 1, 0), nblk - KV_BLOCKS)
        for i in range(Q_ROWS):
            r = g * Q_ROWS + i
            rs = min(max(r - NA_ROWS // 2, 0), rows - NA_ROWS)
            for kr in range(KV_BLOCKS * Q_ROWS):
                key_row = first * Q_ROWS + kr
                if rs <= key_row < rs + NA_ROWS:
                    sel[variant, i, kr] = key_row - r + (NA_ROWS - 1)
    t = planes[:, sel]
    t = jnp.transpose(t, (1, 0, 2, 4, 3, 5))
    return t.reshape(3, h, Q_ROWS * GRID_W, KV_BLOCKS * Q_ROWS * GRID_W).astype(F32)


def _na_kernel(q_ref, k0_ref, k1_ref, k2_ref, v0_ref, v1_ref, v2_ref, kx_ref, vx_ref, bias_ref, o_ref):
    dn = (((1,), (1,)), ((), ()))
    tk = k0_ref.shape[1]
    k_refs = (k0_ref, k1_ref, k2_ref)
    v_refs = (v0_ref, v1_ref, v2_ref)
    for h in range(B_HEADS):
        cols = slice(h * B_DIM, (h + 1) * B_DIM)
        q = q_ref[0, :, cols]
        s = [lax.dot_general(q, k_refs[n][0, :, cols], dn, preferred_element_type=F32)
             + bias_ref[0, h, :, n * tk:(n + 1) * tk] for n in range(KV_BLOCKS)]
        s.append(lax.dot_general(q, kx_ref[0, :, cols], dn, preferred_element_type=F32))
        m = functools.reduce(jnp.maximum, [jnp.max(x, axis=-1, keepdims=True) for x in s])
        p = [jnp.exp(x - m) for x in s]
        l = functools.reduce(jnp.add, [jnp.sum(x, axis=-1, keepdims=True) for x in p])
        o = jnp.dot(p[KV_BLOCKS].astype(BF16), vx_ref[0, :, cols], preferred_element_type=F32)
        for n in range(KV_BLOCKS):
            o += jnp.dot(p[n].astype(BF16), v_refs[n][0, :, cols], preferred_element_type=F32)
        o_ref[0, :, cols] = (o / l).astype(o_ref.dtype)


def _neighborhood_attention(p, pc, bias):
    b, s, _ = p.shape
    lc = pc.shape[1]
    tq = Q_ROWS * GRID_W
    nblk = s // tq
    qcol, kcol, vcol = 2, 3, 4
    first = lambda i: jnp.clip(i - 1, 0, nblk - KV_BLOCKS)
    kv = lambda col, n: pl.BlockSpec((1, tq, B_WIDTH), lambda bi, i: (bi, first(i) + n, col))
    variant = lambda i: jnp.where(i == 0, 0, jnp.where(i == nblk - 1, 2, 1))
    return pl.pallas_call(
        _na_kernel,
        grid=(b, nblk),
        in_specs=[
            pl.BlockSpec((1, tq, B_WIDTH), lambda bi, i: (bi, i, qcol)),
            *[kv(kcol, n) for n in range(KV_BLOCKS)],
            *[kv(vcol, n) for n in range(KV_BLOCKS)],
            pl.BlockSpec((1, lc, B_WIDTH), lambda bi, i: (bi, 0, kcol)),
            pl.BlockSpec((1, lc, B_WIDTH), lambda bi, i: (bi, 0, vcol)),
            pl.BlockSpec((1, *bias.shape[1:]), lambda bi, i: (variant(i), 0, 0, 0)),
        ],
        out_specs=pl.BlockSpec((1, tq, B_WIDTH), lambda bi, i: (bi, i, 0)),
        out_shape=jax.ShapeDtypeStruct((b, s, B_WIDTH), BF16),
        compiler_params=_params(("arbitrary", "arbitrary")),
        name="neighborhood_attention",
    )(p, *([p] * (2 * KV_BLOCKS)), pc, pc, bias)


def _ctx_attn_kernel(q_ref, k_ref, v_ref, o_ref):
    for h in range(B_HEADS):
        cols = slice(h * B_DIM, (h + 1) * B_DIM)
        s = lax.dot_general(q_ref[0, :, cols], k_ref[0, :, cols], (((1,), (1,)), ((), ())),
                            preferred_element_type=F32)
        p = jnp.exp(s - jnp.max(s, axis=-1, keepdims=True))
        o = jnp.dot(p.astype(BF16), v_ref[0, :, cols], preferred_element_type=F32)
        o_ref[0, :, cols] = (o / jnp.sum(p, axis=-1, keepdims=True)).astype(o_ref.dtype)


def _context_attention(pc):
    b, lc, _ = pc.shape
    spec = lambda col: pl.BlockSpec((1, lc, B_WIDTH), lambda bi: (bi, 0, col))
    return pl.pallas_call(
        _ctx_attn_kernel,
        grid=(b,),
        in_specs=[spec(2), spec(3), spec(4)],
        out_specs=pl.BlockSpec((1, lc, B_WIDTH), lambda bi: (bi, 0, 0)),
        out_shape=jax.ShapeDtypeStruct((b, lc, B_WIDTH), BF16),
        compiler_params=_params(("arbitrary",)),
        name="context_attention",
    )(pc, pc, pc)


def _store_residual_and_route(xf_ref, gam_ref, sh_ref, sc_ref, wr_ref, o_ref, h_ref, aff_ref):
    tm = xf_ref.shape[0]
    xn = xf_ref[...]
    h = _norm_mod(xn, gam_ref[...], sh_ref[0], sc_ref[0])
    _slab_zero_pad(o_ref, (0,), tm)
    _slab_zero_pad(h_ref, (0,), tm)
    for c in range(SLAB_ROWS):
        sl = slice(c * LANES, (c + 1) * LANES)
        o_ref[_slab_cols(o_ref, (0,), c, tm)] = xn[:, sl]
        h_ref[_slab_cols(h_ref, (0,), c, tm)] = h[:, sl]
    ne = aff_ref.shape[1]
    h_hi = h.astype(BF16)
    h_lo = (h - h_hi.astype(F32)).astype(BF16)
    p_hi = jnp.dot(h_hi, wr_ref[...], preferred_element_type=F32).T
    p_lo = jnp.dot(h_lo, wr_ref[...], preferred_element_type=F32).T
    logits = p_hi[0:ne] + (p_hi[ne:2 * ne] + p_lo[0:ne])
    m = jnp.max(logits, axis=0, keepdims=True)
    e = jnp.exp(logits - m)
    aff_ref[0] = e / jnp.sum(e, axis=0, keepdims=True)


def _router_weight(router_w):
    d, ne = router_w.shape
    w_hi = router_w.astype(BF16)
    w_lo = (router_w - w_hi.astype(F32)).astype(BF16)
    return jnp.concatenate([w_hi, w_lo, jnp.zeros((d, LANES - 2 * ne), BF16)], axis=1)


def _route_specs(b, n, d, ne, tm):
    slab = pl.BlockSpec((1, tm * PITCH, LANES), lambda bi, i: (bi, i, 0))
    mod = pl.BlockSpec((1, 1, d), lambda bi, i: (bi, 0, 0))
    in_specs = [_resident((1, d)), mod, mod, _resident((d, LANES))]
    out_specs = [slab, slab, pl.BlockSpec((1, ne, tm), lambda bi, i: (bi, 0, i))]
    slab_shape = jax.ShapeDtypeStruct((b, n * PITCH, LANES), F32)
    return in_specs, out_specs, [slab_shape, slab_shape, jax.ShapeDtypeStruct((b, ne, n), F32)]


def _ab_out_kernel(a_ref, b_ref, w_ref, x_ref, g_ref, gam_ref, sh_ref, sc_ref, wr_ref, o_ref, h_ref, aff_ref,
                   xf_ref, *, tn):
    ka = a_ref.shape[2]
    for jt in range(w_ref.shape[1] // tn):
        cols = slice(jt * tn, (jt + 1) * tn)
        acc = jnp.dot(a_ref[0], w_ref[0:ka, cols], preferred_element_type=F32)
        acc += jnp.dot(b_ref[0], w_ref[ka:2 * ka, cols], preferred_element_type=F32)
        xf_ref[:, cols] = x_ref[0, :, cols] + g_ref[0, :, cols] * acc
    _store_residual_and_route(xf_ref, gam_ref, sh_ref, sc_ref, wr_ref, o_ref, h_ref, aff_ref)


def _ab_out(a, bm, w, x, gate, gamma2, shift2, scale2, wr_split, tm, tn=512):
    b, n, d = x.shape
    ka = a.shape[2]
    rin, rout, rshape = _route_specs(b, n, d, N_EXPERTS, tm)
    return pl.pallas_call(
        functools.partial(_ab_out_kernel, tn=tn),
        grid=(b, n // tm),
        in_specs=[
            pl.BlockSpec((1, tm, ka), lambda bi, i: (bi, i, 0)),
            pl.BlockSpec((1, tm, ka), lambda bi, i: (bi, i, 0)),
            _resident(w.shape),
            pl.BlockSpec((1, tm, d), lambda bi, i: (bi, i, 0)),
            pl.BlockSpec((1, 1, d), lambda bi, i: (bi, 0, 0)),
            *rin,
        ],
        out_specs=rout,
        out_shape=rshape,
        scratch_shapes=[pltpu.VMEM((tm, d), F32)],
        compiler_params=_params(("arbitrary", "arbitrary")),
        name="ab_out_proj",
    )(a, bm, w, x, gate, gamma2, shift2, scale2, wr_split)


def _c_in_kernel(x_ref, gam_ref, sh_ref, sc_ref, w_ref, bg_ref, cz_ref, h_ref, xf_ref, *, tn):
    _slab_load_rows(x_ref, (0,), h_ref.shape[0], xf_ref)
    h_ref[...] = _norm_mod(xf_ref[...], gam_ref[...], sh_ref[0], sc_ref[0]).astype(BF16)
    cw = w_ref.shape[1] // 3
    for jt in range(cw // tn):
        cols = slice(jt * tn, (jt + 1) * tn)
        proj = lambda seg: jnp.dot(h_ref[...], w_ref[:, seg * cw + jt * tn:seg * cw + (jt + 1) * tn],
                                   preferred_element_type=F32)
        bg_ref[0, :, cols] = proj(0).astype(bg_ref.dtype)
        cz_ref[0, :, cols] = (proj(1) * proj(2)).astype(cz_ref.dtype)


def _c_in(x, gamma, shift, scale, w, tm, tn=512):
    b = x.shape[0]
    n = x.shape[1] // PITCH
    d = w.shape[0]
    cw = w.shape[1] // 3
    out = jax.ShapeDtypeStruct((b, n, cw), BF16)
    ospec = pl.BlockSpec((1, tm, cw), lambda bi, i: (bi, i, 0))
    return pl.pallas_call(
        functools.partial(_c_in_kernel, tn=tn),
        grid=(b, n // tm),
        in_specs=[
            pl.BlockSpec((1, tm * PITCH, LANES), lambda bi, i: (bi, i, 0)),
            _resident((1, d)),
            pl.BlockSpec((1, 1, d), lambda bi, i: (bi, 0, 0)),
            pl.BlockSpec((1, 1, d), lambda bi, i: (bi, 0, 0)),
            _resident(w.shape),
        ],
        out_specs=[ospec, ospec],
        out_shape=[out, out],
        scratch_shapes=[pltpu.VMEM((tm, d), BF16), pltpu.VMEM((tm, d), F32)],
        compiler_params=_params(("arbitrary", "arbitrary")),
        name="c_in_proj",
    )(x, gamma, shift, scale, w)


HALO = 8


def _c_out_kernel(bg_ref, cz_ref, czp_ref, czn_ref, cw_ref, w_ref, x_ref, g_ref, gam_ref, sh_ref, sc_ref, wr_ref,
                  o_ref, h_ref, aff_ref, xf_ref, acc_ref, *, kchunk, tn):
    i = pl.program_id(1)
    ni = pl.num_programs(1)
    tm = cz_ref.shape[1]
    cw = cz_ref.shape[2]
    _slab_load_rows(x_ref, (0,), tm, xf_ref)
    row = lax.broadcasted_iota(jnp.int32, (tm, kchunk), 0)
    for kc in range(cw // kchunk):
        ks = slice(kc * kchunk, (kc + 1) * kchunk)
        cz = cz_ref[0, :, ks].astype(F32)
        prev_row = jnp.where(i > 0, czp_ref[0, HALO - 1:HALO, ks].astype(F32), 0.0)
        next_row = jnp.where(i < ni - 1, czn_ref[0, 0:1, ks].astype(F32), 0.0)
        up = jnp.where(row == 0, prev_row, pltpu.roll(cz, 1, 0))
        dn = jnp.where(row == tm - 1, next_row, pltpu.roll(cz, tm - 1, 0))
        y = cw_ref[0:1, ks] * up + cw_ref[1:2, ks] * cz + cw_ref[2:3, ks] * dn
        lhs = (bg_ref[0, :, ks].astype(F32) * y).astype(BF16)
        for jt in range(w_ref.shape[1] // tn):
            cols = slice(jt * tn, (jt + 1) * tn)
            part = jnp.dot(lhs, w_ref[ks, cols], preferred_element_type=F32)
            if kc == 0:
                acc_ref[:, cols] = part
            else:
                acc_ref[:, cols] += part
    xf_ref[...] = xf_ref[...] + g_ref[0] * acc_ref[...]
    _store_residual_and_route(xf_ref, gam_ref, sh_ref, sc_ref, wr_ref, o_ref, h_ref, aff_ref)


def _c_out(bg, cz, conv_w, w, x, gate, gamma2, shift2, scale2, wr_split, tm, tn=512):
    b, n, cw = bg.shape
    d = w.shape[1]
    hb = tm // HALO
    nh = n // HALO
    rin, rout, rshape = _route_specs(b, n, d, N_EXPERTS, tm)
    return pl.pallas_call(
        functools.partial(_c_out_kernel, kchunk=512, tn=tn),
        grid=(b, n // tm),
        in_specs=[
            pl.BlockSpec((1, tm, cw), lambda bi, i: (bi, i, 0)),
            pl.BlockSpec((1, tm, cw), lambda bi, i: (bi, i, 0)),
            pl.BlockSpec((1, HALO, cw), lambda bi, i: (bi, jnp.maximum(i * hb - 1, 0), 0)),
            pl.BlockSpec((1, HALO, cw), lambda bi, i: (bi, jnp.minimum((i + 1) * hb, nh - 1), 0)),
            _resident((CONV_W, cw)),
            _resident(w.shape),
            pl.BlockSpec((1, tm * PITCH, LANES), lambda bi, i: (bi, i, 0)),
            pl.BlockSpec((1, 1, d), lambda bi, i: (bi, 0, 0)),
            *rin,
        ],
        out_specs=rout,
        out_shape=rshape,
        scratch_shapes=[pltpu.VMEM((tm, d), F32), pltpu.VMEM((tm, d), F32)],
        compiler_params=_params(("arbitrary", "arbitrary")),
        name="c_out_proj",
    )(bg, cz, cz, cz, conv_w, w, x, gate, gamma2, shift2, scale2, wr_split)


SEARCH_BITS = 3


def _select_kernel(aff_ref, idx_ref, gate_ref, *, cap, cchunk):
    v = aff_ref[0, 0]
    nr = v.shape[0]

    def count(mask):
        return jnp.sum(jnp.sum(mask.astype(F32), axis=1, keepdims=True), axis=0, keepdims=True)

    def search(it, t):
        shift = 30 - SEARCH_BITS * (it + 1)
        digit = jnp.zeros((1, 1), jnp.int32)
        for j in range(1, 2 ** SEARCH_BITS):
            cand = pltpu.bitcast(t | (jnp.int32(j) << shift), F32)
            digit += (count(v >= cand) >= cap).astype(jnp.int32)
        return t | (digit << shift)

    thr = pltpu.bitcast(lax.fori_loop(0, 30 // SEARCH_BITS, search, jnp.zeros((1, 1), jnp.int32)), F32)
    gt = v > thr
    eq = v == thr
    need = cap - count(gt)

    lane_l = lax.broadcasted_iota(jnp.int32, (LANES, LANES), 0)
    lane_c = lax.broadcasted_iota(jnp.int32, (LANES, LANES), 1)
    tri_lane = (lane_l <= lane_c).astype(BF16)
    row_r = lax.broadcasted_iota(jnp.int32, (nr, nr), 0)
    row_c = lax.broadcasted_iota(jnp.int32, (nr, nr), 1)
    tri_row = (row_c <= row_r).astype(BF16)

    def prefix(mask):
        mf = mask.astype(BF16)
        in_row = jnp.dot(mf, tri_lane, preferred_element_type=F32)
        colcum = jnp.dot(tri_row, mf, preferred_element_type=F32)
        row_incl = jnp.sum(colcum, axis=1, keepdims=True)
        row_tot = jnp.sum(mask.astype(F32), axis=1, keepdims=True)
        return in_row, row_incl - row_tot, row_incl

    eq_in, eq_off, _ = prefix(eq)
    eq_rank = eq_in + eq_off - eq.astype(F32)
    sel = gt | (eq & (eq_rank < need))
    _, sel_off, sel_incl = prefix(sel)

    self_bf = sel.astype(BF16)
    tri_lane_t = (lane_c <= lane_l).astype(BF16)
    dn_t = (((1,), (1,)), ((), ()))
    pt = lax.dot_general(tri_lane_t, self_bf, dn_t, preferred_element_type=F32)
    eye = (lane_l == lane_c).astype(F32)
    vt = lax.dot_general(eye, v, dn_t, precision=HIGHEST, preferred_element_type=F32)

    for c0 in range(0, cap, cchunk):
        cc = min(cchunk, cap - c0)
        slot = (lax.broadcasted_iota(jnp.int32, (1, cc), 1) + c0).astype(F32)
        r_of = jnp.sum((sel_incl <= slot).astype(F32), axis=0, keepdims=True)
        onehot = (lax.broadcasted_iota(jnp.int32, (nr, cc), 0).astype(F32) == r_of)
        onehot_f = onehot.astype(F32)
        local = slot - jnp.sum(onehot_f * sel_off, axis=0, keepdims=True)
        prow = jnp.dot(pt.astype(BF16), onehot.astype(BF16), preferred_element_type=F32)
        l_of = jnp.sum((prow <= local).astype(F32), axis=0, keepdims=True)
        vrow = jnp.dot(vt, onehot_f, precision=HIGHEST, preferred_element_type=F32)
        lane_i = lax.broadcasted_iota(jnp.int32, (LANES, cc), 0).astype(F32)
        gsel = jnp.sum(jnp.where(lane_i == l_of, vrow, 0.0), axis=0, keepdims=True)
        idx_ref[0, 0, :, c0:c0 + cc] = (r_of * LANES + l_of).astype(jnp.int32)
        gate_ref[0, 0, :, c0:c0 + cc] = gsel


MIN_SELECT_ROWS = 8


def _select(aff_t, cap):
    b, ne, n = aff_t.shape
    if n < MIN_SELECT_ROWS * LANES:
        assert cap <= n
        aff_t = jnp.pad(aff_t, ((0, 0), (0, 0), (0, MIN_SELECT_ROWS * LANES - n)), constant_values=-1.0)
        n = MIN_SELECT_ROWS * LANES
    nr = n // LANES
    out = lambda dt: jax.ShapeDtypeStruct((b, ne, 1, cap), dt)
    ospec = pl.BlockSpec((1, 1, 1, cap), lambda bi, e: (bi, e, 0, 0))
    idx, gate = pl.pallas_call(
        functools.partial(_select_kernel, cap=cap, cchunk=512),
        grid=(b, ne),
        in_specs=[pl.BlockSpec((1, 1, nr, LANES), lambda bi, e: (bi, e, 0, 0))],
        out_specs=[ospec, ospec],
        out_shape=[out(jnp.int32), out(F32)],
        compiler_params=_params(("arbitrary", "arbitrary")),
        name="moe_select",
    )(aff_t.reshape(b, ne, nr, LANES))
    return idx.reshape(b, ne, cap), gate.reshape(b, ne, cap)


X_SLOTS = 2
O_SLOTS = 3
ISSUE_UNROLL = 8


def _moe_kernel(idx_ref, idxn_ref, g_ref, g2_ref, h_hbm, wg_hbm, wu_hbm, wd_hbm, x_hbm, o_hbm,
                xg, og, xs_ref, wg_buf, wu_buf, wd_buf, wg_stage, wu_stage, wd_stage, sem_x, sem_o, sem_s, sem_w,
                *, n_tok, layer, nb, nt):
    del x_hbm
    e, bi, t = pl.program_id(0), pl.program_id(1), pl.program_id(2)
    ne = pl.num_programs(0)
    step = (e * nb + bi) * nt + t
    last = ne * nb * nt - 1
    tc = xs_ref.shape[0]
    moved = tc * SLAB_ROWS

    chunks = nb * nt
    chunk = bi * nt + t
    wslot = e % 2
    stages = (wg_stage, wu_stage, wd_stage)
    bufs = (wg_buf, wu_buf, wd_buf)

    def weight_rows(ref, k):
        rows = ref.shape[0]
        return pl.ds(pl.multiple_of(k * rows, rows), rows)

    def weight_copies(expert, k):
        return [pltpu.make_async_copy(hbm.at[layer, expert, weight_rows(stage, k), :], stage, sem_w.at[i])
                for i, (hbm, stage) in enumerate(zip((wg_hbm, wu_hbm, wd_hbm), stages))]

    def cast_chunk(slot, k):
        for stage, buf in zip(stages, bufs):
            buf[slot, weight_rows(stage, k), :] = stage[...].astype(BF16)

    @pl.when(step == 0)
    def _():
        def load(k, carry):
            copies = weight_copies(0, k)
            for cp in copies:
                cp.start()
            for cp in copies:
                cp.wait()
            cast_chunk(0, k)
            return carry
        lax.fori_loop(0, chunks, load, 0)

    @pl.when(e + 1 < ne)
    def _():
        for cp in weight_copies(e + 1, chunk):
            cp.start()

    def for_each_slot_row(idx, sample, fn):
        def body(s8, carry):
            for u in range(ISSUE_UNROLL):
                s = s8 * ISSUE_UNROLL + u
                src = pl.multiple_of((sample * n_tok + idx[0, 0, s]) * PITCH, 4)
                fn(pl.ds(src, SLAB_ROWS), pl.ds(pl.multiple_of(s * PITCH, 4), SLAB_ROWS))
            return carry
        lax.fori_loop(0, tc // ISSUE_UNROLL, body, 0)

    def issue_gathers(idx, sample, xslot, oslot):
        def one(src, dst):
            pltpu.make_async_copy(h_hbm.at[src, :], xg.at[xslot, dst, :], sem_x.at[xslot]).start()
            pltpu.make_async_copy(o_hbm.at[src, :], og.at[oslot, dst, :], sem_o.at[oslot]).start()
        for_each_slot_row(idx, sample, one)

    def wait_rows(hbm, buf, slot, sem, to_hbm):
        a, b = hbm.at[pl.ds(0, moved), :], buf.at[slot, pl.ds(0, moved), :]
        (pltpu.make_async_copy(b, a, sem.at[slot]) if to_hbm else pltpu.make_async_copy(a, b, sem.at[slot])).wait()

    @pl.when(step == 0)
    def _():
        issue_gathers(idx_ref, bi, 0, 0)

    @pl.when(step >= 2)
    def _():
        wait_rows(o_hbm, og, (step - 2) % O_SLOTS, sem_s, True)

    @pl.when(step < last)
    def _():
        nxt = step + 1
        issue_gathers(idxn_ref, (nxt // nt) % nb, nxt % X_SLOTS, nxt % O_SLOTS)

    xslot = step % X_SLOTS
    oslot = step % O_SLOTS
    wait_rows(h_hbm, xg, xslot, sem_x, False)
    for c in range(SLAB_ROWS):
        xs_ref[:, c * LANES:(c + 1) * LANES] = xg[xslot, pl.ds(c, tc, stride=PITCH), :].astype(BF16)
    xs = xs_ref[...]
    gate = jnp.dot(xs, wg_buf[wslot], preferred_element_type=F32)
    up = jnp.dot(xs, wu_buf[wslot], preferred_element_type=F32)
    hid = (_silu(gate) * up).astype(BF16)
    y = jnp.dot(hid, wd_buf[wslot], preferred_element_type=F32)
    eye = lax.broadcasted_iota(jnp.int32, (tc, tc), 0) == lax.broadcasted_iota(jnp.int32, (tc, tc), 1)
    gcol = jnp.sum(jnp.where(eye, g_ref[0], 0.0), axis=1, keepdims=True)
    y = y * gcol

    wait_rows(o_hbm, og, oslot, sem_o, False)
    for c in range(SLAB_ROWS):
        cols = slice(c * LANES, (c + 1) * LANES)
        rows = (oslot, pl.ds(c, tc, stride=PITCH), slice(None))
        og[rows] = og[rows] + g2_ref[0][:, cols] * y[:, cols]

    def scatter(src, dst):
        pltpu.make_async_copy(og.at[oslot, dst, :], o_hbm.at[src, :], sem_s.at[oslot]).start()
    for_each_slot_row(idx_ref, bi, scatter)

    @pl.when(e + 1 < ne)
    def _():
        for cp in weight_copies(e + 1, chunk):
            cp.wait()
        cast_chunk(1 - wslot, chunk)

    @pl.when(step == last)
    def _():
        @pl.when(step >= 1)
        def _():
            wait_rows(o_hbm, og, (step - 1) % O_SLOTS, sem_s, True)
        wait_rows(o_hbm, og, oslot, sem_s, True)


def _moe_experts(idx, gate, h, wg, wu, wd, layer, gate2, x, tc):
    b, ne, cap = idx.shape
    n = x.shape[1] // PITCH
    d, f = wg.shape[2], wg.shape[3]
    nt = cap // tc
    assert b >= 2 and nt >= 2 and tc % ISSUE_UNROLL == 0 and d == SLAB_ROWS * LANES
    nsteps = ne * b * nt

    def cur(e, bi, t):
        return ((bi * ne + e) * nt + t, 0, 0)

    def nxt(e, bi, t):
        step = jnp.minimum((e * b + bi) * nt + t + 1, nsteps - 1)
        return (((step // nt) % b * ne + step // (nt * b)) * nt + step % nt, 0, 0)

    idx3 = idx.reshape(b * ne * nt, 1, tc)
    rows = (tc * PITCH, LANES)
    chunks = b * nt
    assert d % (16 * chunks) == 0 and f % (16 * chunks) == 0
    hbm = pl.BlockSpec(memory_space=pl.ANY)
    out = pl.pallas_call(
        functools.partial(_moe_kernel, n_tok=n, layer=layer, nb=b, nt=nt),
        grid=(ne, b, nt),
        in_specs=[
            pl.BlockSpec((1, 1, tc), cur, memory_space=pltpu.SMEM),
            pl.BlockSpec((1, 1, tc), nxt, memory_space=pltpu.SMEM),
            pl.BlockSpec((1, 1, tc), cur),
            pl.BlockSpec((1, 1, d), lambda e, bi, t: (bi, 0, 0)),
            hbm, hbm, hbm, hbm, hbm,
        ],
        out_specs=hbm,
        out_shape=jax.ShapeDtypeStruct((b * n * PITCH, LANES), F32),
        input_output_aliases={8: 0},
        scratch_shapes=[
            pltpu.VMEM((X_SLOTS, *rows), F32), pltpu.VMEM((O_SLOTS, *rows), F32), pltpu.VMEM((tc, d), BF16),
            pltpu.VMEM((2, d, f), BF16), pltpu.VMEM((2, d, f), BF16), pltpu.VMEM((2, f, d), BF16),
            pltpu.VMEM((d // chunks, f), F32), pltpu.VMEM((d // chunks, f), F32), pltpu.VMEM((f // chunks, d), F32),
            pltpu.SemaphoreType.DMA((X_SLOTS,)), pltpu.SemaphoreType.DMA((O_SLOTS,)),
            pltpu.SemaphoreType.DMA((O_SLOTS,)), pltpu.SemaphoreType.DMA((3,)),
        ],
        compiler_params=_params(("arbitrary", "arbitrary", "arbitrary"), disable_bounds_checks=True),
        name="moe_experts",
    )(idx3, idx3, gate.reshape(b * ne * nt, 1, tc), gate2, h.reshape(b * n * PITCH, LANES), wg, wu, wd,
      x.reshape(b * n * PITCH, LANES))
    return out.reshape(x.shape)


def _ec_moe_residual(x, h, aff_t, gate2, wg, wu, wd, layer, *, tc):
    n = x.shape[1] // PITCH
    cap = CAP_FACTOR * n // N_EXPERTS
    idx, g = _select(aff_t, cap)
    return _moe_experts(idx, g, h, wg, wu, wd, layer, gate2, x, tc)


def _slab_to_std_kernel(x_ref, o_ref):
    _slab_load_rows(x_ref, (0,), o_ref.shape[1], o_ref.at[0])


def _slab_to_std(x, d, tm):
    b = x.shape[0]
    n = x.shape[1] // PITCH
    return pl.pallas_call(
        _slab_to_std_kernel,
        grid=(b, n // tm),
        in_specs=[pl.BlockSpec((1, tm * PITCH, LANES), lambda bi, i: (bi, i, 0))],
        out_specs=pl.BlockSpec((1, tm, d), lambda bi, i: (bi, i, 0)),
        out_shape=jax.ShapeDtypeStruct((b, n, d), F32),
        compiler_params=_params(("arbitrary", "arbitrary")),
        name="slab_to_std",
    )(x)


CTX_MOE_TILE = 16


def kernel(x, c, ctx, c_ctx, ada_w, ada_b, norm1_g, norm2_g, ab_w_in, ab_w_out, a_ws, a_bs, a_vnorm_g,
           b_qnorm_g, b_knorm_g, b_rpb, c_w_in, c_conv_w, c_w_out, router_w, moe_w_gate, moe_w_up, moe_w_down):
    bsz, seq, d = x.shape
    lc = ctx.shape[1]
    depth = ada_w.shape[0]
    assert depth == 2, "layer plan: mixer A/B layer (updates the context stream) then mixer C layer"

    cond = jnp.concatenate([c, c_ctx[None], jnp.zeros((8 - bsz - 1, d), F32)], axis=0)
    mod = _ada(cond, ada_w, ada_b)

    wg_all, wu_all, wd_all = moe_w_gate, moe_w_up, moe_w_down
    xl = x
    xc = ctx
    for i in range(depth):
        upd_ctx = i < depth - 1
        sh1, sc1, g1, sh2, sc2, g2 = [mod[i, :bsz, k * d:(k + 1) * d].reshape(bsz, 1, d) for k in range(6)]
        csh1, csc1, cg1, csh2, csc2, cg2 = [
            jnp.broadcast_to(mod[i, bsz, k * d:(k + 1) * d].reshape(1, 1, d), (bsz, 1, d)) for k in range(6)]
        gam1 = norm1_g[i].reshape(1, d)
        gam2 = norm2_g[i].reshape(1, d)
        wr_split = _router_weight(router_w[i])
        j = i // 2
        if i % 2 == 0:
            w_in = ab_w_in[j].astype(BF16)
            w_out = ab_w_out[j].astype(BF16)
            ones = jnp.ones((A_WIDTH,), F32)
            gain = jnp.concatenate([
                ones, a_vnorm_g[j].reshape(-1),
                jnp.tile(b_qnorm_g[j], B_HEADS) * (B_DIM ** -0.5),
                jnp.tile(b_knorm_g[j], B_HEADS), ones]).reshape(1, -1)
            p = _ab_in(xl, gam1, sh1, sc1, w_in, gain, tm=512)
            pc = _ab_in(xc, gam1, csh1, csc1, w_in, gain, tm=lc)
            ws = a_ws[j].astype(BF16)
            bs_b = jnp.broadcast_to(a_bs[j][:, :, None], (A_GROUPS, CHUNK, A_DIM)).astype(F32)
            a_l = _gmlp(p, ws, bs_b, tm=512)
            b_l = _neighborhood_attention(p, pc, _na_bias_table(b_rpb[j], seq // GRID_W))
            xl, h2, aff_t = _ab_out(a_l, b_l, w_out, xl, g1, gam2, sh2, sc2, wr_split, tm=512)
            if upd_ctx:
                a_c = _gmlp(pc, ws, bs_b, tm=lc)
                b_c = _context_attention(pc)
                xc, hc2, affc_t = _ab_out(a_c, b_c, w_out, xc, cg1, gam2, csh2, csc2, wr_split, tm=lc)
        else:
            assert not upd_ctx
            bg, cz = _c_in(xl, gam1, sh1, sc1, c_w_in[j].astype(BF16), tm=512)
            xl, h2, aff_t = _c_out(bg, cz, c_conv_w[j], c_w_out[j].astype(BF16), xl, g1, gam2, sh2, sc2,
                                   wr_split, tm=512)

        xl = _ec_moe_residual(xl, h2, aff_t, g2, wg_all, wu_all, wd_all, i, tc=256)
        if upd_ctx:
            xc = _ec_moe_residual(xc, hc2, affc_t, cg2, wg_all, wu_all, wd_all, i, tc=CTX_MOE_TILE)
            xc = _slab_to_std(xc, d, tm=lc)
    return _slab_to_std(xl, d, tm=512)
```

```python
import functools
import math

import jax
import jax.numpy as jnp
import numpy as np
from jax import lax
from jax.experimental import pallas as pl
from jax.experimental.pallas import tpu as pltpu

GRID_W = 64
CHUNK = 128
A_GROUPS = 8
A_DIM = 128
A_WIDTH = A_GROUPS * A_DIM
B_HEADS = 8
B_DIM = 128
B_WIDTH = B_HEADS * B_DIM
NA_ROWS = 8
NA_COLS = 16
CONV_W = 3
N_EXPERTS = 16
CAP_FACTOR = 2
EPS = 1e-6

LANES = 128
VMEM_LIMIT = 56 * 1024 * 1024

F32 = jnp.float32
BF16 = jnp.bfloat16
HIGHEST = lax.Precision.HIGHEST
MASK_VALUE = -1e30


SLAB_ROWS = 16
PITCH = 20


def _params(sem, **kw):
    return pltpu.CompilerParams(dimension_semantics=sem, vmem_limit_bytes=VMEM_LIMIT, **kw)


def _slab_cols(ref, lead, c, tm):
    return (*lead, pl.ds(c, tm, stride=PITCH), slice(None))


def _slab_zero_pad(ref, lead, tm):
    for c in range(SLAB_ROWS, PITCH):
        ref[_slab_cols(ref, lead, c, tm)] = jnp.zeros((tm, LANES), ref.dtype)


def _slab_load_rows(ref, lead, tm, dst_ref):
    for c in range(SLAB_ROWS):
        dst_ref[:, c * LANES:(c + 1) * LANES] = ref[_slab_cols(ref, lead, c, tm)]


def _silu(x):
    return x * (1.0 / (1.0 + jnp.exp(-x)))


def _gelu_tanh(x):
    return 0.5 * x * (1.0 + jnp.tanh(math.sqrt(2.0 / math.pi) * (x + 0.044715 * (x * x * x))))


def _norm_mod(x, gamma, shift, scale):
    ms = jnp.mean(x * x, axis=-1, keepdims=True)
    return (x * lax.rsqrt(ms + EPS) * gamma) * (1.0 + scale) + shift


def _ada_kernel(c_ref, w_ref, b_ref, o_ref):
    s = _silu(c_ref[...])
    o_ref[0] = jnp.dot(s, w_ref[0], precision=HIGHEST, preferred_element_type=F32) + b_ref[0]


def _ada(cond, ada_w, ada_b):
    depth, d, n6 = ada_w.shape
    tn = 1024
    return pl.pallas_call(
        _ada_kernel,
        grid=(depth, n6 // tn),
        in_specs=[
            pl.BlockSpec((8, d), lambda l, j: (0, 0)),
            pl.BlockSpec((1, d, tn), lambda l, j: (l, 0, j)),
            pl.BlockSpec((1, 1, tn), lambda l, j: (l, 0, j)),
        ],
        out_specs=pl.BlockSpec((1, 8, tn), lambda l, j: (l, 0, j)),
        out_shape=jax.ShapeDtypeStruct((depth, 8, n6), F32),
        compiler_params=_params(("arbitrary", "arbitrary")),
        name="ada_mod",
    )(cond, ada_w, ada_b.reshape(depth, 1, n6))


AB_SEGMENT_EPILOGUES = ("gelu", "gelu_norm", "norm", "norm", "none")


def _ab_in_kernel(x_ref, gam_ref, sh_ref, sc_ref, w_ref, gain_ref, o_ref, h_ref, *, tn):
    h_ref[...] = _norm_mod(x_ref[0], gam_ref[...], sh_ref[0], sc_ref[0]).astype(BF16)
    per_seg = A_WIDTH // tn
    for jt in range(w_ref.shape[1] // tn):
        kind = AB_SEGMENT_EPILOGUES[jt // per_seg]
        acc = jnp.dot(h_ref[...], w_ref[:, jt * tn:(jt + 1) * tn], preferred_element_type=F32)
        if kind.startswith("gelu"):
            acc = _gelu_tanh(acc)
        if kind.endswith("norm"):
            for g in range(tn // LANES):
                sl = slice(jt * tn + g * LANES, jt * tn + (g + 1) * LANES)
                ag = acc[:, g * LANES:(g + 1) * LANES]
                ms = jnp.mean(ag * ag, axis=-1, keepdims=True)
                o_ref[0, :, sl] = (ag * lax.rsqrt(ms + EPS) * gain_ref[:, sl]).astype(o_ref.dtype)
        else:
            o_ref[0, :, jt * tn:(jt + 1) * tn] = acc.astype(o_ref.dtype)


def _resident(shape):
    nd = len(shape)
    return pl.BlockSpec(shape, lambda *_: (0,) * nd, pipeline_mode=pl.Buffered(1))


def _ab_in(x, gamma, shift, scale, w, gain, tm, tn=512):
    b, n, d = x.shape
    f = w.shape[1]
    return pl.pallas_call(
        functools.partial(_ab_in_kernel, tn=tn),
        grid=(b, n // tm),
        in_specs=[
            pl.BlockSpec((1, tm, d), lambda bi, i: (bi, i, 0)),
            _resident((1, d)),
            pl.BlockSpec((1, 1, d), lambda bi, i: (bi, 0, 0)),
            pl.BlockSpec((1, 1, d), lambda bi, i: (bi, 0, 0)),
            _resident((d, f)),
            _resident((1, f)),
        ],
        out_specs=pl.BlockSpec((1, tm, f), lambda bi, i: (bi, i, 0)),
        out_shape=jax.ShapeDtypeStruct((b, n, f), BF16),
        scratch_shapes=[pltpu.VMEM((tm, d), BF16)],
        compiler_params=_params(("arbitrary", "arbitrary")),
        name="ab_in_proj",
    )(x, gamma, shift, scale, w, gain)


def _gmlp_tile(u_ref, v_ref, ws_ref, bs_ref, a_ref):
    for ch in range(a_ref.shape[0] // CHUNK):
        rows = slice(ch * CHUNK, (ch + 1) * CHUNK)
        for g in range(A_GROUPS):
            cols = slice(g * A_DIM, (g + 1) * A_DIM)
            s = jnp.dot(ws_ref[g], v_ref[0, rows, cols], preferred_element_type=F32) + bs_ref[g]
            a_ref[rows, cols] = (u_ref[0, rows, cols].astype(F32) * s).astype(a_ref.dtype)


Q_ROWS = 4
KV_BLOCKS = 3
MASKED_PLANE = 2 * NA_ROWS - 1


def _na_bias_table(rpb, rows):
    h = rpb.shape[0]
    cols = np.arange(GRID_W)
    cstart = np.clip(cols - NA_COLS // 2, 0, GRID_W - NA_COLS)
    valid = (cols[None, :] >= cstart[:, None]) & (cols[None, :] < cstart[:, None] + NA_COLS)
    col_off = np.clip(cols[None, :] - cols[:, None] + (NA_COLS - 1), 0, 2 * NA_COLS - 2)
    planes = jnp.where(valid[None, None], rpb[:, :, col_off], MASK_VALUE)
    planes = jnp.concatenate([planes, jnp.full((h, 1, GRID_W, GRID_W), MASK_VALUE, planes.dtype)], axis=1)
    nblk = rows // Q_ROWS
    assert nblk >= KV_BLOCKS + 1 and NA_ROWS <= (KV_BLOCKS - 1) * Q_ROWS
    sel = np.full((3, Q_ROWS, KV_BLOCKS * Q_ROWS), MASKED_PLANE, np.int32)
    for variant, g in enumerate((0, 1, nblk - 1)):
        first = min(max(g - 1, 0), nblk - KV_BLOCKS)
        for i in range(Q_ROWS):
            r = g * Q_ROWS + i
            rs = min(max(r - NA_ROWS // 2, 0), rows - NA_ROWS)
            for kr in range(KV_BLOCKS * Q_ROWS):
                key_row = first * Q_ROWS + kr
                if rs <= key_row < rs + NA_ROWS:
                    sel[variant, i, kr] = key_row - r + (NA_ROWS - 1)
    t = planes[:, sel]
    t = jnp.transpose(t, (1, 0, 2, 4, 3, 5))
    return t.reshape(3, h, Q_ROWS * GRID_W, KV_BLOCKS * Q_ROWS * GRID_W).astype(F32)


def _na_kernel(q_ref, k0_ref, k1_ref, k2_ref, v0_ref, v1_ref, v2_ref, kx_ref, vx_ref, bias_ref, o_ref):
    dn = (((1,), (1,)), ((), ()))
    tk = k0_ref.shape[1]
    k_refs = (k0_ref, k1_ref, k2_ref)
    v_refs = (v0_ref, v1_ref, v2_ref)
    for h in range(B_HEADS):
        cols = slice(h * B_DIM, (h + 1) * B_DIM)
        q = q_ref[0, :, cols]
        s = [lax.dot_general(q, k_refs[n][0, :, cols], dn, preferred_element_type=F32)
             + bias_ref[0, h, :, n * tk:(n + 1) * tk] for n in range(KV_BLOCKS)]
        s.append(lax.dot_general(q, kx_ref[0, :, cols], dn, preferred_element_type=F32))
        m = functools.reduce(jnp.maximum, [jnp.max(x, axis=-1, keepdims=True) for x in s])
        p = [jnp.exp(x - m) for x in s]
        l = functools.reduce(jnp.add, [jnp.sum(x, axis=-1, keepdims=True) for x in p])
        o = jnp.dot(p[KV_BLOCKS].astype(BF16), vx_ref[0, :, cols], preferred_element_type=F32)
        for n in range(KV_BLOCKS):
            o += jnp.dot(p[n].astype(BF16), v_refs[n][0, :, cols], preferred_element_type=F32)
        o_ref[0, :, cols] = (o / l).astype(o_ref.dtype)


def _neighborhood_attention(p, pc, bias):
    b, s, _ = p.shape
    lc = pc.shape[1]
    tq = Q_ROWS * GRID_W
    nblk = s // tq
    qcol, kcol, vcol = 2, 3, 4
    first = lambda i: jnp.clip(i - 1, 0, nblk - KV_BLOCKS)
    kv = lambda col, n: pl.BlockSpec((1, tq, B_WIDTH), lambda bi, i: (bi, first(i) + n, col))
    variant = lambda i: jnp.where(i == 0, 0, jnp.where(i == nblk - 1, 2, 1))
    return pl.pallas_call(
        _na_kernel,
        grid=(b, nblk),
        in_specs=[
            pl.BlockSpec((1, tq, B_WIDTH), lambda bi, i: (bi, i, qcol)),
            *[kv(kcol, n) for n in range(KV_BLOCKS)],
            *[kv(vcol, n) for n in range(KV_BLOCKS)],
            pl.BlockSpec((1, lc, B_WIDTH), lambda bi, i: (bi, 0, kcol)),
            pl.BlockSpec((1, lc, B_WIDTH), lambda bi, i: (bi, 0, vcol)),
            pl.BlockSpec((1, *bias.shape[1:]), lambda bi, i: (variant(i), 0, 0, 0)),
        ],
        out_specs=pl.BlockSpec((1, tq, B_WIDTH), lambda bi, i: (bi, i, 0)),
        out_shape=jax.ShapeDtypeStruct((b, s, B_WIDTH), BF16),
        compiler_params=_params(("arbitrary", "arbitrary")),
        name="neighborhood_attention",
    )(p, *([p] * (2 * KV_BLOCKS)), pc, pc, bias)


def _ctx_attn_kernel(q_ref, k_ref, v_ref, o_ref):
    for h in range(B_HEADS):
        cols = slice(h * B_DIM, (h + 1) * B_DIM)
        s = lax.dot_general(q_ref[0, :, cols], k_ref[0, :, cols], (((1,), (1,)), ((), ())),
                            preferred_element_type=F32)
        p = jnp.exp(s - jnp.max(s, axis=-1, keepdims=True))
        o = jnp.dot(p.astype(BF16), v_ref[0, :, cols], preferred_element_type=F32)
        o_ref[0, :, cols] = (o / jnp.sum(p, axis=-1, keepdims=True)).astype(o_ref.dtype)


def _context_attention(pc):
    b, lc, _ = pc.shape
    spec = lambda col: pl.BlockSpec((1, lc, B_WIDTH), lambda bi: (bi, 0, col))
    return pl.pallas_call(
        _ctx_attn_kernel,
        grid=(b,),
        in_specs=[spec(2), spec(3), spec(4)],
        out_specs=pl.BlockSpec((1, lc, B_WIDTH), lambda bi: (bi, 0, 0)),
        out_shape=jax.ShapeDtypeStruct((b, lc, B_WIDTH), BF16),
        compiler_params=_params(("arbitrary",)),
        name="context_attention",
    )(pc, pc, pc)


def _store_residual_and_route(xf_ref, gam_ref, sh_ref, sc_ref, wr_ref, o_ref, h_ref, aff_ref):
    tm = xf_ref.shape[0]
    xn = xf_ref[...]
    h = _norm_mod(xn, gam_ref[...], sh_ref[0], sc_ref[0])
    _slab_zero_pad(o_ref, (0,), tm)
    _slab_zero_pad(h_ref, (0,), tm)
    for c in range(SLAB_ROWS):
        sl = slice(c * LANES, (c + 1) * LANES)
        o_ref[_slab_cols(o_ref, (0,), c, tm)] = xn[:, sl]
        h_ref[_slab_cols(h_ref, (0,), c, tm)] = h[:, sl]
    ne = aff_ref.shape[1]
    h_hi = h.astype(BF16)
    h_lo = (h - h_hi.astype(F32)).astype(BF16)
    p_hi = jnp.dot(h_hi, wr_ref[...], preferred_element_type=F32).T
    p_lo = jnp.dot(h_lo, wr_ref[...], preferred_element_type=F32).T
    logits = p_hi[0:ne] + (p_hi[ne:2 * ne] + p_lo[0:ne])
    m = jnp.max(logits, axis=0, keepdims=True)
    e = jnp.exp(logits - m)
    aff_ref[0] = e / jnp.sum(e, axis=0, keepdims=True)


def _router_weight(router_w):
    d, ne = router_w.shape
    w_hi = router_w.astype(BF16)
    w_lo = (router_w - w_hi.astype(F32)).astype(BF16)
    return jnp.concatenate([w_hi, w_lo, jnp.zeros((d, LANES - 2 * ne), BF16)], axis=1)


def _route_specs(b, n, d, ne, tm):
    slab = pl.BlockSpec((1, tm * PITCH, LANES), lambda bi, i: (bi, i, 0))
    mod = pl.BlockSpec((1, 1, d), lambda bi, i: (bi, 0, 0))
    in_specs = [_resident((1, d)), mod, mod, _resident((d, LANES))]
    out_specs = [slab, slab, pl.BlockSpec((1, ne, tm), lambda bi, i: (bi, 0, i))]
    slab_shape = jax.ShapeDtypeStruct((b, n * PITCH, LANES), F32)
    return in_specs, out_specs, [slab_shape, slab_shape, jax.ShapeDtypeStruct((b, ne, n), F32)]


def _ab_out_kernel(u_ref, v_ref, ws_ref, bs_ref, b_ref, w_ref, x_ref, g_ref, gam_ref, sh_ref, sc_ref, wr_ref,
                   o_ref, h_ref, aff_ref, xf_ref, a_ref, *, tn):
    _gmlp_tile(u_ref, v_ref, ws_ref, bs_ref, a_ref)
    ka = a_ref.shape[1]
    for jt in range(w_ref.shape[1] // tn):
        cols = slice(jt * tn, (jt + 1) * tn)
        acc = jnp.dot(a_ref[...], w_ref[0:ka, cols], preferred_element_type=F32)
        acc += jnp.dot(b_ref[0], w_ref[ka:2 * ka, cols], preferred_element_type=F32)
        xf_ref[:, cols] = x_ref[0, :, cols] + g_ref[0, :, cols] * acc
    _store_residual_and_route(xf_ref, gam_ref, sh_ref, sc_ref, wr_ref, o_ref, h_ref, aff_ref)


def _ab_out(p, ws, bs_b, bm, w, x, gate, gamma2, shift2, scale2, wr_split, tm, tn=512):
    b, n, d = x.shape
    rin, rout, rshape = _route_specs(b, n, d, N_EXPERTS, tm)
    return pl.pallas_call(
        functools.partial(_ab_out_kernel, tn=tn),
        grid=(b, n // tm),
        in_specs=[
            pl.BlockSpec((1, tm, A_WIDTH), lambda bi, i: (bi, i, 0)),
            pl.BlockSpec((1, tm, A_WIDTH), lambda bi, i: (bi, i, 1)),
            _resident(ws.shape),
            _resident(bs_b.shape),
            pl.BlockSpec((1, tm, B_WIDTH), lambda bi, i: (bi, i, 0)),
            _resident(w.shape),
            pl.BlockSpec((1, tm, d), lambda bi, i: (bi, i, 0)),
            pl.BlockSpec((1, 1, d), lambda bi, i: (bi, 0, 0)),
            *rin,
        ],
        out_specs=rout,
        out_shape=rshape,
        scratch_shapes=[pltpu.VMEM((tm, d), F32), pltpu.VMEM((tm, A_WIDTH), BF16)],
        compiler_params=_params(("arbitrary", "arbitrary")),
        name="ab_out_proj",
    )(p, p, ws, bs_b, bm, w, x, gate, gamma2, shift2, scale2, wr_split)


def _c_in_kernel(x_ref, gam_ref, sh_ref, sc_ref, w_ref, bg_ref, cz_ref, h_ref, xf_ref, *, tn):
    _slab_load_rows(x_ref, (0,), h_ref.shape[0], xf_ref)
    h_ref[...] = _norm_mod(xf_ref[...], gam_ref[...], sh_ref[0], sc_ref[0]).astype(BF16)
    cw = w_ref.shape[1] // 3
    for jt in range(cw // tn):
        cols = slice(jt * tn, (jt + 1) * tn)
        proj = lambda seg: jnp.dot(h_ref[...], w_ref[:, seg * cw + jt * tn:seg * cw + (jt + 1) * tn],
                                   preferred_element_type=F32)
        bg_ref[0, :, cols] = proj(0).astype(bg_ref.dtype)
        cz_ref[0, :, cols] = (proj(1) * proj(2)).astype(cz_ref.dtype)


def _c_in(x, gamma, shift, scale, w, tm, tn=512):
    b = x.shape[0]
    n = x.shape[1] // PITCH
    d = w.shape[0]
    cw = w.shape[1] // 3
    out = jax.ShapeDtypeStruct((b, n, cw), BF16)
    ospec = pl.BlockSpec((1, tm, cw), lambda bi, i: (bi, i, 0))
    return pl.pallas_call(
        functools.partial(_c_in_kernel, tn=tn),
        grid=(b, n // tm),
        in_specs=[
            pl.BlockSpec((1, tm * PITCH, LANES), lambda bi, i: (bi, i, 0)),
            _resident((1, d)),
            pl.BlockSpec((1, 1, d), lambda bi, i: (bi, 0, 0)),
            pl.BlockSpec((1, 1, d), lambda bi, i: (bi, 0, 0)),
            _resident(w.shape),
        ],
        out_specs=[ospec, ospec],
        out_shape=[out, out],
        scratch_shapes=[pltpu.VMEM((tm, d), BF16), pltpu.VMEM((tm, d), F32)],
        compiler_params=_params(("arbitrary", "arbitrary")),
        name="c_in_proj",
    )(x, gamma, shift, scale, w)


HALO = 8


def _c_out_kernel(bg_ref, cz_ref, czp_ref, czn_ref, cw_ref, w_ref, x_ref, g_ref, gam_ref, sh_ref, sc_ref, wr_ref,
                  o_ref, h_ref, aff_ref, xf_ref, acc_ref, *, kchunk, tn):
    i = pl.program_id(1)
    ni = pl.num_programs(1)
    tm = cz_ref.shape[1]
    cw = cz_ref.shape[2]
    _slab_load_rows(x_ref, (0,), tm, xf_ref)
    row = lax.broadcasted_iota(jnp.int32, (tm, kchunk), 0)
    for kc in range(cw // kchunk):
        ks = slice(kc * kchunk, (kc + 1) * kchunk)
        cz = cz_ref[0, :, ks].astype(F32)
        prev_row = jnp.where(i > 0, czp_ref[0, HALO - 1:HALO, ks].astype(F32), 0.0)
        next_row = jnp.where(i < ni - 1, czn_ref[0, 0:1, ks].astype(F32), 0.0)
        up = jnp.where(row == 0, prev_row, pltpu.roll(cz, 1, 0))
        dn = jnp.where(row == tm - 1, next_row, pltpu.roll(cz, tm - 1, 0))
        y = cw_ref[0:1, ks] * up + cw_ref[1:2, ks] * cz + cw_ref[2:3, ks] * dn
        lhs = (bg_ref[0, :, ks].astype(F32) * y).astype(BF16)
        for jt in range(w_ref.shape[1] // tn):
            cols = slice(jt * tn, (jt + 1) * tn)
            part = jnp.dot(lhs, w_ref[ks, cols], preferred_element_type=F32)
            if kc == 0:
                acc_ref[:, cols] = part
            else:
                acc_ref[:, cols] += part
    xf_ref[...] = xf_ref[...] + g_ref[0] * acc_ref[...]
    _store_residual_and_route(xf_ref, gam_ref, sh_ref, sc_ref, wr_ref, o_ref, h_ref, aff_ref)


def _c_out(bg, cz, conv_w, w, x, gate, gamma2, shift2, scale2, wr_split, tm, tn=512):
    b, n, cw = bg.shape
    d = w.shape[1]
    hb = tm // HALO
    nh = n // HALO
    rin, rout, rshape = _route_specs(b, n, d, N_EXPERTS, tm)
    return pl.pallas_call(
        functools.partial(_c_out_kernel, kchunk=512, tn=tn),
        grid=(b, n // tm),
        in_specs=[
            pl.BlockSpec((1, tm, cw), lambda bi, i: (bi, i, 0)),
            pl.BlockSpec((1, tm, cw), lambda bi, i: (bi, i, 0)),
            pl.BlockSpec((1, HALO, cw), lambda bi, i: (bi, jnp.maximum(i * hb - 1, 0), 0)),
            pl.BlockSpec((1, HALO, cw), lambda bi, i: (bi, jnp.minimum((i + 1) * hb, nh - 1), 0)),
            _resident((CONV_W, cw)),
            _resident(w.shape),
            pl.BlockSpec((1, tm * PITCH, LANES), lambda bi, i: (bi, i, 0)),
            pl.BlockSpec((1, 1, d), lambda bi, i: (bi, 0, 0)),
            *rin,
        ],
        out_specs=rout,
        out_shape=rshape,
        scratch_shapes=[pltpu.VMEM((tm, d), F32), pltpu.VMEM((tm, d), F32)],
        compiler_params=_params(("arbitrary", "arbitrary")),
        name="c_out_proj",
    )(bg, cz, cz, cz, conv_w, w, x, gate, gamma2, shift2, scale2, wr_split)


SEARCH_BITS = 3
SELECT_GROUP = 4


def _select_kernel(aff_ref, idx_ref, gate_ref, *, cap, cchunk):
    group = aff_ref.shape[1]
    vs = [aff_ref[0, g] for g in range(group)]

    def count(mask):
        return jnp.sum(jnp.sum(mask.astype(F32), axis=1, keepdims=True), axis=0, keepdims=True)

    def search(it, ts):
        shift = 30 - SEARCH_BITS * (it + 1)
        out = []
        for v, t in zip(vs, ts):
            digit = jnp.zeros((1, 1), jnp.int32)
            for j in range(1, 2 ** SEARCH_BITS):
                cand = pltpu.bitcast(t | (jnp.int32(j) << shift), F32)
                digit += (count(v >= cand) >= cap).astype(jnp.int32)
            out.append(t | (digit << shift))
        return tuple(out)

    thrs = lax.fori_loop(0, 30 // SEARCH_BITS, search, tuple(jnp.zeros((1, 1), jnp.int32) for _ in vs))
    for g, (v, thr_bits) in enumerate(zip(vs, thrs)):
        _compact_selected(v, pltpu.bitcast(thr_bits, F32), idx_ref.at[0, g], gate_ref.at[0, g], cap, cchunk, count)


def _compact_selected(v, thr, idx_ref, gate_ref, cap, cchunk, count):
    nr = v.shape[0]
    gt = v > thr
    eq = v == thr
    need = cap - count(gt)

    lane_l = lax.broadcasted_iota(jnp.int32, (LANES, LANES), 0)
    lane_c = lax.broadcasted_iota(jnp.int32, (LANES, LANES), 1)
    tri_lane = (lane_l <= lane_c).astype(BF16)
    row_r = lax.broadcasted_iota(jnp.int32, (nr, nr), 0)
    row_c = lax.broadcasted_iota(jnp.int32, (nr, nr), 1)
    tri_row = (row_c <= row_r).astype(BF16)

    def prefix(mask):
        mf = mask.astype(BF16)
        in_row = jnp.dot(mf, tri_lane, preferred_element_type=F32)
        colcum = jnp.dot(tri_row, mf, preferred_element_type=F32)
        row_incl = jnp.sum(colcum, axis=1, keepdims=True)
        row_tot = jnp.sum(mask.astype(F32), axis=1, keepdims=True)
        return in_row, row_incl - row_tot, row_incl

    eq_in, eq_off, _ = prefix(eq)
    eq_rank = eq_in + eq_off - eq.astype(F32)
    sel = gt | (eq & (eq_rank < need))
    _, sel_off, sel_incl = prefix(sel)

    self_bf = sel.astype(BF16)
    tri_lane_t = (lane_c <= lane_l).astype(BF16)
    dn_t = (((1,), (1,)), ((), ()))
    pt = lax.dot_general(tri_lane_t, self_bf, dn_t, preferred_element_type=F32)
    eye = (lane_l == lane_c).astype(F32)
    vt = lax.dot_general(eye, v, dn_t, precision=HIGHEST, preferred_element_type=F32)

    for c0 in range(0, cap, cchunk):
        cc = min(cchunk, cap - c0)
        slot = (lax.broadcasted_iota(jnp.int32, (1, cc), 1) + c0).astype(F32)
        r_of = jnp.sum((sel_incl <= slot).astype(F32), axis=0, keepdims=True)
        onehot = (lax.broadcasted_iota(jnp.int32, (nr, cc), 0).astype(F32) == r_of)
        onehot_f = onehot.astype(F32)
        local = slot - jnp.sum(onehot_f * sel_off, axis=0, keepdims=True)
        prow = jnp.dot(pt.astype(BF16), onehot.astype(BF16), preferred_element_type=F32)
        l_of = jnp.sum((prow <= local).astype(F32), axis=0, keepdims=True)
        vrow = jnp.dot(vt, onehot_f, precision=HIGHEST, preferred_element_type=F32)
        lane_i = lax.broadcasted_iota(jnp.int32, (LANES, cc), 0).astype(F32)
        gsel = jnp.sum(jnp.where(lane_i == l_of, vrow, 0.0), axis=0, keepdims=True)
        idx_ref[:, c0:c0 + cc] = (r_of * LANES + l_of).astype(jnp.int32)
        gate_ref[:, c0:c0 + cc] = gsel


MIN_SELECT_ROWS = 8


def _select(aff_t, cap):
    b, ne, n = aff_t.shape
    if n < MIN_SELECT_ROWS * LANES:
        assert cap <= n
        aff_t = jnp.pad(aff_t, ((0, 0), (0, 0), (0, MIN_SELECT_ROWS * LANES - n)), constant_values=-1.0)
        n = MIN_SELECT_ROWS * LANES
    nr = n // LANES
    group = SELECT_GROUP
    assert ne % group == 0
    out = lambda dt: jax.ShapeDtypeStruct((b, ne, 1, cap), dt)
    ospec = pl.BlockSpec((1, group, 1, cap), lambda bi, e: (bi, e, 0, 0))
    idx, gate = pl.pallas_call(
        functools.partial(_select_kernel, cap=cap, cchunk=512),
        grid=(b, ne // group),
        in_specs=[pl.BlockSpec((1, group, nr, LANES), lambda bi, e: (bi, e, 0, 0))],
        out_specs=[ospec, ospec],
        out_shape=[out(jnp.int32), out(F32)],
        compiler_params=_params(("arbitrary", "arbitrary")),
        name="moe_select",
    )(aff_t.reshape(b, ne, nr, LANES))
    return idx.reshape(b, ne, cap), gate.reshape(b, ne, cap)


X_SLOTS = 2
O_SLOTS = 3
ISSUE_UNROLL = 8


def _moe_kernel(idx_ref, idxn_ref, g_ref, g2_ref, h_hbm, wg_hbm, wu_hbm, wd_hbm, x_hbm, o_hbm,
                xg, og, xs_ref, wg_buf, wu_buf, wd_buf, wg_stage, wu_stage, wd_stage, sem_x, sem_o, sem_s, sem_w,
                *, n_tok, layer, nb, nt):
    del x_hbm
    e, bi, t = pl.program_id(0), pl.program_id(1), pl.program_id(2)
    ne = pl.num_programs(0)
    step = (e * nb + bi) * nt + t
    last = ne * nb * nt - 1
    tc = xs_ref.shape[0]
    moved = tc * SLAB_ROWS

    chunks = nb * nt
    chunk = bi * nt + t
    wslot = e % 2
    stages = (wg_stage, wu_stage, wd_stage)
    bufs = (wg_buf, wu_buf, wd_buf)

    def weight_rows(ref, k):
        rows = ref.shape[0]
        return pl.ds(pl.multiple_of(k * rows, rows), rows)

    def weight_copies(expert, k):
        return [pltpu.make_async_copy(hbm.at[layer, expert, weight_rows(stage, k), :], stage, sem_w.at[i])
                for i, (hbm, stage) in enumerate(zip((wg_hbm, wu_hbm, wd_hbm), stages))]

    def cast_chunk(slot, k):
        for stage, buf in zip(stages, bufs):
            buf[slot, weight_rows(stage, k), :] = stage[...].astype(BF16)

    @pl.when(step == 0)
    def _():
        def load(k, carry):
            copies = weight_copies(0, k)
            for cp in copies:
                cp.start()
            for cp in copies:
                cp.wait()
            cast_chunk(0, k)
            return carry
        lax.fori_loop(0, chunks, load, 0)

    @pl.when(e + 1 < ne)
    def _():
        for cp in weight_copies(e + 1, chunk):
            cp.start()

    def for_each_slot_row(idx, sample, fn):
        def body(s8, carry):
            for u in range(ISSUE_UNROLL):
                s = s8 * ISSUE_UNROLL + u
                src = pl.multiple_of((sample * n_tok + idx[0, 0, s]) * PITCH, 4)
                fn(pl.ds(src, SLAB_ROWS), pl.ds(pl.multiple_of(s * PITCH, 4), SLAB_ROWS))
            return carry
        lax.fori_loop(0, tc // ISSUE_UNROLL, body, 0)

    def issue_gathers(idx, sample, xslot, oslot):
        def one(src, dst):
            pltpu.make_async_copy(h_hbm.at[src, :], xg.at[xslot, dst, :], sem_x.at[xslot]).start()
            pltpu.make_async_copy(o_hbm.at[src, :], og.at[oslot, dst, :], sem_o.at[oslot]).start()
        for_each_slot_row(idx, sample, one)

    def wait_rows(hbm, buf, slot, sem, to_hbm):
        a, b = hbm.at[pl.ds(0, moved), :], buf.at[slot, pl.ds(0, moved), :]
        (pltpu.make_async_copy(b, a, sem.at[slot]) if to_hbm else pltpu.make_async_copy(a, b, sem.at[slot])).wait()

    @pl.when(step == 0)
    def _():
        issue_gathers(idx_ref, bi, 0, 0)

    @pl.when(step >= 2)
    def _():
        wait_rows(o_hbm, og, (step - 2) % O_SLOTS, sem_s, True)

    xslot = step % X_SLOTS
    oslot = step % O_SLOTS
    wait_rows(h_hbm, xg, xslot, sem_x, False)
    for c in range(SLAB_ROWS):
        xs_ref[:, c * LANES:(c + 1) * LANES] = xg[xslot, pl.ds(c, tc, stride=PITCH), :].astype(BF16)

    nxt = (step + 1) % (last + 1)
    nsample = (nxt // nt) % nb
    nxslot = (step + 1) % X_SLOTS
    noslot = (step + 1) % O_SLOTS
    for s in range(tc):
        src = pl.ds(pl.multiple_of((nsample * n_tok + idxn_ref[0, 0, s]) * PITCH, 4), SLAB_ROWS)
        dst = pl.ds(s * PITCH, SLAB_ROWS)
        pltpu.make_async_copy(h_hbm.at[src, :], xg.at[nxslot, dst, :], sem_x.at[nxslot]).start()
        pltpu.make_async_copy(o_hbm.at[src, :], og.at[noslot, dst, :], sem_o.at[noslot]).start()

    xs = xs_ref[...]
    gate = jnp.dot(xs, wg_buf[wslot], preferred_element_type=F32)
    up = jnp.dot(xs, wu_buf[wslot], preferred_element_type=F32)
    hid = (_silu(gate) * up).astype(BF16)
    y = jnp.dot(hid, wd_buf[wslot], preferred_element_type=F32)
    eye = lax.broadcasted_iota(jnp.int32, (tc, tc), 0) == lax.broadcasted_iota(jnp.int32, (tc, tc), 1)
    gcol = jnp.sum(jnp.where(eye, g_ref[0], 0.0), axis=1, keepdims=True)
    y = y * gcol

    wait_rows(o_hbm, og, oslot, sem_o, False)
    for c in range(SLAB_ROWS):
        cols = slice(c * LANES, (c + 1) * LANES)
        rows = (oslot, pl.ds(c, tc, stride=PITCH), slice(None))
        og[rows] = og[rows] + g2_ref[0][:, cols] * y[:, cols]

    for s in range(tc):
        dst = pl.ds(pl.multiple_of((bi * n_tok + idx_ref[0, 0, s]) * PITCH, 4), SLAB_ROWS)
        pltpu.make_async_copy(og.at[oslot, pl.ds(s * PITCH, SLAB_ROWS), :], o_hbm.at[dst, :], sem_s.at[oslot]).start()

    @pl.when(e + 1 < ne)
    def _():
        for cp in weight_copies(e + 1, chunk):
            cp.wait()
        cast_chunk(1 - wslot, chunk)

    @pl.when(step == last)
    def _():
        @pl.when(step >= 1)
        def _():
            wait_rows(o_hbm, og, (step - 1) % O_SLOTS, sem_s, True)
        wait_rows(o_hbm, og, oslot, sem_s, True)
        wait_rows(h_hbm, xg, nxslot, sem_x, False)
        wait_rows(o_hbm, og, noslot, sem_o, False)


def _moe_experts(idx, gate, h, wg, wu, wd, layer, gate2, x, tc):
    b, ne, cap = idx.shape
    n = x.shape[1] // PITCH
    d, f = wg.shape[2], wg.shape[3]
    nt = cap // tc
    assert b >= 2 and nt >= 2 and tc % ISSUE_UNROLL == 0 and d == SLAB_ROWS * LANES
    nsteps = ne * b * nt

    def cur(e, bi, t):
        return ((bi * ne + e) * nt + t, 0, 0)

    def nxt(e, bi, t):
        step = ((e * b + bi) * nt + t + 1) % nsteps
        return (((step // nt) % b * ne + step // (nt * b)) * nt + step % nt, 0, 0)

    idx3 = idx.reshape(b * ne * nt, 1, tc)
    rows = (tc * PITCH, LANES)
    chunks = b * nt
    assert d % (16 * chunks) == 0 and f % (16 * chunks) == 0
    hbm = pl.BlockSpec(memory_space=pl.ANY)
    out = pl.pallas_call(
        functools.partial(_moe_kernel, n_tok=n, layer=layer, nb=b, nt=nt),
        grid=(ne, b, nt),
        in_specs=[
            pl.BlockSpec((1, 1, tc), cur, memory_space=pltpu.SMEM),
            pl.BlockSpec((1, 1, tc), nxt, memory_space=pltpu.SMEM),
            pl.BlockSpec((1, 1, tc), cur),
            pl.BlockSpec((1, 1, d), lambda e, bi, t: (bi, 0, 0)),
            hbm, hbm, hbm, hbm, hbm,
        ],
        out_specs=hbm,
        out_shape=jax.ShapeDtypeStruct((b * n * PITCH, LANES), F32),
        input_output_aliases={8: 0},
        scratch_shapes=[
            pltpu.VMEM((X_SLOTS, *rows), F32), pltpu.VMEM((O_SLOTS, *rows), F32), pltpu.VMEM((tc, d), BF16),
            pltpu.VMEM((2, d, f), BF16), pltpu.VMEM((2, d, f), BF16), pltpu.VMEM((2, f, d), BF16),
            pltpu.VMEM((d // chunks, f), F32), pltpu.VMEM((d // chunks, f), F32), pltpu.VMEM((f // chunks, d), F32),
            pltpu.SemaphoreType.DMA((X_SLOTS,)), pltpu.SemaphoreType.DMA((O_SLOTS,)),
            pltpu.SemaphoreType.DMA((O_SLOTS,)), pltpu.SemaphoreType.DMA((3,)),
        ],
        compiler_params=_params(("arbitrary", "arbitrary", "arbitrary"), disable_bounds_checks=True),
        name="moe_experts",
    )(idx3, idx3, gate.reshape(b * ne * nt, 1, tc), gate2, h.reshape(b * n * PITCH, LANES), wg, wu, wd,
      x.reshape(b * n * PITCH, LANES))
    return out.reshape(x.shape)


def _ec_moe_residual(x, h, aff_t, gate2, wg, wu, wd, layer, *, tc):
    n = x.shape[1] // PITCH
    cap = CAP_FACTOR * n // N_EXPERTS
    idx, g = _select(aff_t, cap)
    return _moe_experts(idx, g, h, wg, wu, wd, layer, gate2, x, tc)


def _slab_to_std_kernel(x_ref, o_ref):
    _slab_load_rows(x_ref, (0,), o_ref.shape[1], o_ref.at[0])


def _slab_to_std(x, d, tm):
    b = x.shape[0]
    n = x.shape[1] // PITCH
    return pl.pallas_call(
        _slab_to_std_kernel,
        grid=(b, n // tm),
        in_specs=[pl.BlockSpec((1, tm * PITCH, LANES), lambda bi, i: (bi, i, 0))],
        out_specs=pl.BlockSpec((1, tm, d), lambda bi, i: (bi, i, 0)),
        out_shape=jax.ShapeDtypeStruct((b, n, d), F32),
        compiler_params=_params(("arbitrary", "arbitrary")),
        name="slab_to_std",
    )(x)


CTX_MOE_TILE = 16


def kernel(x, c, ctx, c_ctx, ada_w, ada_b, norm1_g, norm2_g, ab_w_in, ab_w_out, a_ws, a_bs, a_vnorm_g,
           b_qnorm_g, b_knorm_g, b_rpb, c_w_in, c_conv_w, c_w_out, router_w, moe_w_gate, moe_w_up, moe_w_down):
    bsz, seq, d = x.shape
    lc = ctx.shape[1]
    depth = ada_w.shape[0]
    assert depth == 2, "layer plan: mixer A/B layer (updates the context stream) then mixer C layer"

    cond = jnp.concatenate([c, c_ctx[None], jnp.zeros((8 - bsz - 1, d), F32)], axis=0)
    mod = _ada(cond, ada_w, ada_b)

    wg_all, wu_all, wd_all = moe_w_gate, moe_w_up, moe_w_down
    xl = x
    xc = ctx
    for i in range(depth):
        upd_ctx = i < depth - 1
        sh1, sc1, g1, sh2, sc2, g2 = [mod[i, :bsz, k * d:(k + 1) * d].reshape(bsz, 1, d) for k in range(6)]
        csh1, csc1, cg1, csh2, csc2, cg2 = [
            jnp.broadcast_to(mod[i, bsz, k * d:(k + 1) * d].reshape(1, 1, d), (bsz, 1, d)) for k in range(6)]
        gam1 = norm1_g[i].reshape(1, d)
        gam2 = norm2_g[i].reshape(1, d)
        wr_split = _router_weight(router_w[i])
        j = i // 2
        if i % 2 == 0:
            w_in = ab_w_in[j].astype(BF16)
            w_out = ab_w_out[j].astype(BF16)
            ones = jnp.ones((A_WIDTH,), F32)
            gain = jnp.concatenate([
                ones, a_vnorm_g[j].reshape(-1),
                jnp.tile(b_qnorm_g[j], B_HEADS) * (B_DIM ** -0.5),
                jnp.tile(b_knorm_g[j], B_HEADS), ones]).reshape(1, -1)
            p = _ab_in(xl, gam1, sh1, sc1, w_in, gain, tm=512)
            pc = _ab_in(xc, gam1, csh1, csc1, w_in, gain, tm=lc)
            ws = a_ws[j].astype(BF16)
            bs_b = jnp.broadcast_to(a_bs[j][:, :, None], (A_GROUPS, CHUNK, A_DIM)).astype(F32)
            b_l = _neighborhood_attention(p, pc, _na_bias_table(b_rpb[j], seq // GRID_W))
            xl, h2, aff_t = _ab_out(p, ws, bs_b, b_l, w_out, xl, g1, gam2, sh2, sc2, wr_split, tm=512)
            if upd_ctx:
                b_c = _context_attention(pc)
                xc, hc2, affc_t = _ab_out(pc, ws, bs_b, b_c, w_out, xc, cg1, gam2, csh2, csc2, wr_split, tm=lc)
        else:
            assert not upd_ctx
            bg, cz = _c_in(xl, gam1, sh1, sc1, c_w_in[j].astype(BF16), tm=512)
            xl, h2, aff_t = _c_out(bg, cz, c_conv_w[j], c_w_out[j].astype(BF16), xl, g1, gam2, sh2, sc2,
                                   wr_split, tm=512)

        xl = _ec_moe_residual(xl, h2, aff_t, g2, wg_all, wu_all, wd_all, i, tc=256)
        if upd_ctx:
            xc = _ec_moe_residual(xc, hc2, affc_t, cg2, wg_all, wu_all, wd_all, i, tc=CTX_MOE_TILE)
            xc = _slab_to_std(xc, d, tm=lc)
    return _slab_to_std(xl, d, tm=512)
```

```python
import functools
import math

import jax
import jax.numpy as jnp
import numpy as np
from jax import lax
from jax.experimental import pallas as pl
from jax.experimental.pallas import tpu as pltpu

GRID_W = 64
CHUNK = 128
A_GROUPS = 8
A_DIM = 128
A_WIDTH = A_GROUPS * A_DIM
B_HEADS = 8
B_DIM = 128
B_WIDTH = B_HEADS * B_DIM
NA_ROWS = 8
NA_COLS = 16
CONV_W = 3
N_EXPERTS = 16
CAP_FACTOR = 2
EPS = 1e-6

LANES = 128
VMEM_LIMIT = 56 * 1024 * 1024

F32 = jnp.float32
BF16 = jnp.bfloat16
HIGHEST = lax.Precision.HIGHEST
MASK_VALUE = -1e30


SLAB_ROWS = 16
PITCH = 20


def _params(sem, **kw):
    return pltpu.CompilerParams(dimension_semantics=sem, vmem_limit_bytes=VMEM_LIMIT, **kw)


def _slab_cols(ref, lead, c, tm):
    return (*lead, pl.ds(c, tm, stride=PITCH), slice(None))


def _slab_zero_pad(ref, lead, tm):
    for c in range(SLAB_ROWS, PITCH):
        ref[_slab_cols(ref, lead, c, tm)] = jnp.zeros((tm, LANES), ref.dtype)


def _slab_load_rows(ref, lead, tm, dst_ref):
    for c in range(SLAB_ROWS):
        dst_ref[:, c * LANES:(c + 1) * LANES] = ref[_slab_cols(ref, lead, c, tm)]


def _silu(x):
    return x * (1.0 / (1.0 + jnp.exp(-x)))


def _gelu_tanh(x):
    return 0.5 * x * (1.0 + jnp.tanh(math.sqrt(2.0 / math.pi) * (x + 0.044715 * (x * x * x))))


def _norm_mod(x, gamma, shift, scale):
    ms = jnp.mean(x * x, axis=-1, keepdims=True)
    return (x * lax.rsqrt(ms + EPS) * gamma) * (1.0 + scale) + shift


def _ada_kernel(c_ref, w_ref, b_ref, o_ref):
    s = _silu(c_ref[...])
    o_ref[0] = jnp.dot(s, w_ref[0], precision=HIGHEST, preferred_element_type=F32) + b_ref[0]


def _ada(cond, ada_w, ada_b):
    depth, d, n6 = ada_w.shape
    tn = 1024
    return pl.pallas_call(
        _ada_kernel,
        grid=(depth, n6 // tn),
        in_specs=[
            pl.BlockSpec((8, d), lambda l, j: (0, 0)),
            pl.BlockSpec((1, d, tn), lambda l, j: (l, 0, j)),
            pl.BlockSpec((1, 1, tn), lambda l, j: (l, 0, j)),
        ],
        out_specs=pl.BlockSpec((1, 8, tn), lambda l, j: (l, 0, j)),
        out_shape=jax.ShapeDtypeStruct((depth, 8, n6), F32),
        compiler_params=_params(("arbitrary", "arbitrary")),
        name="ada_mod",
    )(cond, ada_w, ada_b.reshape(depth, 1, n6))


AB_SEGMENT_EPILOGUES = ("gelu", "gelu_norm", "norm", "norm", "none")


def _ab_in_kernel(x_ref, gam_ref, sh_ref, sc_ref, w_ref, gain_ref, o_ref, h_ref, *, tn):
    h_ref[...] = _norm_mod(x_ref[0], gam_ref[...], sh_ref[0], sc_ref[0]).astype(BF16)
    per_seg = A_WIDTH // tn
    for jt in range(w_ref.shape[1] // tn):
        kind = AB_SEGMENT_EPILOGUES[jt // per_seg]
        acc = jnp.dot(h_ref[...], w_ref[:, jt * tn:(jt + 1) * tn], preferred_element_type=F32)
        if kind.startswith("gelu"):
            acc = _gelu_tanh(acc)
        if kind.endswith("norm"):
            for g in range(tn // LANES):
                sl = slice(jt * tn + g * LANES, jt * tn + (g + 1) * LANES)
                ag = acc[:, g * LANES:(g + 1) * LANES]
                ms = jnp.mean(ag * ag, axis=-1, keepdims=True)
                o_ref[0, :, sl] = (ag * lax.rsqrt(ms + EPS) * gain_ref[:, sl]).astype(o_ref.dtype)
        else:
            o_ref[0, :, jt * tn:(jt + 1) * tn] = acc.astype(o_ref.dtype)


def _resident(shape):
    nd = len(shape)
    return pl.BlockSpec(shape, lambda *_: (0,) * nd, pipeline_mode=pl.Buffered(1))


def _ab_in(x, gamma, shift, scale, w, gain, tm, tn=512):
    b, n, d = x.shape
    f = w.shape[1]
    return pl.pallas_call(
        functools.partial(_ab_in_kernel, tn=tn),
        grid=(b, n // tm),
        in_specs=[
            pl.BlockSpec((1, tm, d), lambda bi, i: (bi, i, 0)),
            _resident((1, d)),
            pl.BlockSpec((1, 1, d), lambda bi, i: (bi, 0, 0)),
            pl.BlockSpec((1, 1, d), lambda bi, i: (bi, 0, 0)),
            _resident((d, f)),
            _resident((1, f)),
        ],
        out_specs=pl.BlockSpec((1, tm, f), lambda bi, i: (bi, i, 0)),
        out_shape=jax.ShapeDtypeStruct((b, n, f), BF16),
        scratch_shapes=[pltpu.VMEM((tm, d), BF16)],
        compiler_params=_params(("arbitrary", "arbitrary")),
        name="ab_in_proj",
    )(x, gamma, shift, scale, w, gain)


def _gmlp_tile(u_ref, v_ref, ws_ref, bs_ref, a_ref):
    for ch in range(a_ref.shape[0] // CHUNK):
        rows = slice(ch * CHUNK, (ch + 1) * CHUNK)
        for g in range(A_GROUPS):
            cols = slice(g * A_DIM, (g + 1) * A_DIM)
            s = jnp.dot(ws_ref[g], v_ref[0, rows, cols], preferred_element_type=F32) + bs_ref[g]
            a_ref[rows, cols] = (u_ref[0, rows, cols].astype(F32) * s).astype(a_ref.dtype)


Q_ROWS = 4
KV_BLOCKS = 3
MASKED_PLANE = 2 * NA_ROWS - 1


def _na_bias_table(rpb, rows):
    h = rpb.shape[0]
    cols = np.arange(GRID_W)
    cstart = np.clip(cols - NA_COLS // 2, 0, GRID_W - NA_COLS)
    valid = (cols[None, :] >= cstart[:, None]) & (cols[None, :] < cstart[:, None] + NA_COLS)
    col_off = np.clip(cols[None, :] - cols[:, None] + (NA_COLS - 1), 0, 2 * NA_COLS - 2)
    planes = jnp.where(valid[None, None], rpb[:, :, col_off], MASK_VALUE)
    planes = jnp.concatenate([planes, jnp.full((h, 1, GRID_W, GRID_W), MASK_VALUE, planes.dtype)], axis=1)
    nblk = rows // Q_ROWS
    assert nblk >= KV_BLOCKS + 1 and NA_ROWS <= (KV_BLOCKS - 1) * Q_ROWS
    sel = np.full((3, Q_ROWS, KV_BLOCKS * Q_ROWS), MASKED_PLANE, np.int32)
    for variant, g in enumerate((0, 1, nblk - 1)):
        first = min(max(g - 1, 0), nblk - KV_BLOCKS)
        for i in range(Q_ROWS):
            r = g * Q_ROWS + i
            rs = min(max(r - NA_ROWS // 2, 0), rows - NA_ROWS)
            for kr in range(KV_BLOCKS * Q_ROWS):
                key_row = first * Q_ROWS + kr
                if rs <= key_row < rs + NA_ROWS:
                    sel[variant, i, kr] = key_row - r + (NA_ROWS - 1)
    t = planes[:, sel]
    t = jnp.transpose(t, (1, 0, 2, 4, 3, 5))
    return t.reshape(3, h, Q_ROWS * GRID_W, KV_BLOCKS * Q_ROWS * GRID_W).astype(F32)


def _na_kernel(q_ref, k0_ref, k1_ref, k2_ref, v0_ref, v1_ref, v2_ref, kx_ref, vx_ref, bias_ref, o_ref):
    dn = (((1,), (1,)), ((), ()))
    tk = k0_ref.shape[1]
    k_refs = (k0_ref, k1_ref, k2_ref)
    v_refs = (v0_ref, v1_ref, v2_ref)
    for h in range(B_HEADS):
        cols = slice(h * B_DIM, (h + 1) * B_DIM)
        q = q_ref[0, :, cols]
        s = [lax.dot_general(q, k_refs[n][0, :, cols], dn, preferred_element_type=F32)
             + bias_ref[0, h, :, n * tk:(n + 1) * tk] for n in range(KV_BLOCKS)]
        s.append(lax.dot_general(q, kx_ref[0, :, cols], dn, preferred_element_type=F32))
        m = functools.reduce(jnp.maximum, [jnp.max(x, axis=-1, keepdims=True) for x in s])
        p = [jnp.exp(x - m) for x in s]
        l = functools.reduce(jnp.add, [jnp.sum(x, axis=-1, keepdims=True) for x in p])
        o = jnp.dot(p[KV_BLOCKS].astype(BF16), vx_ref[0, :, cols], preferred_element_type=F32)
        for n in range(KV_BLOCKS):
            o += jnp.dot(p[n].astype(BF16), v_refs[n][0, :, cols], preferred_element_type=F32)
        o_ref[0, :, cols] = (o / l).astype(o_ref.dtype)


def _neighborhood_attention(p, pc, bias):
    b, s, _ = p.shape
    lc = pc.shape[1]
    tq = Q_ROWS * GRID_W
    nblk = s // tq
    qcol, kcol, vcol = 2, 3, 4
    first = lambda i: jnp.clip(i - 1, 0, nblk - KV_BLOCKS)
    kv = lambda col, n: pl.BlockSpec((1, tq, B_WIDTH), lambda bi, i: (bi, first(i) + n, col))
    variant = lambda i: jnp.where(i == 0, 0, jnp.where(i == nblk - 1, 2, 1))
    return pl.pallas_call(
        _na_kernel,
        grid=(b, nblk),
        in_specs=[
            pl.BlockSpec((1, tq, B_WIDTH), lambda bi, i: (bi, i, qcol)),
            *[kv(kcol, n) for n in range(KV_BLOCKS)],
            *[kv(vcol, n) for n in range(KV_BLOCKS)],
            pl.BlockSpec((1, lc, B_WIDTH), lambda bi, i: (bi, 0, kcol)),
            pl.BlockSpec((1, lc, B_WIDTH), lambda bi, i: (bi, 0, vcol)),
            pl.BlockSpec((1, *bias.shape[1:]), lambda bi, i: (variant(i), 0, 0, 0)),
        ],
        out_specs=pl.BlockSpec((1, tq, B_WIDTH), lambda bi, i: (bi, i, 0)),
        out_shape=jax.ShapeDtypeStruct((b, s, B_WIDTH), BF16),
        compiler_params=_params(("arbitrary", "arbitrary")),
        name="neighborhood_attention",
    )(p, *([p] * (2 * KV_BLOCKS)), pc, pc, bias)


def _ctx_attn_kernel(q_ref, k_ref, v_ref, o_ref):
    for h in range(B_HEADS):
        cols = slice(h * B_DIM, (h + 1) * B_DIM)
        s = lax.dot_general(q_ref[0, :, cols], k_ref[0, :, cols], (((1,), (1,)), ((), ())),
                            preferred_element_type=F32)
        p = jnp.exp(s - jnp.max(s, axis=-1, keepdims=True))
        o = jnp.dot(p.astype(BF16), v_ref[0, :, cols], preferred_element_type=F32)
        o_ref[0, :, cols] = (o / jnp.sum(p, axis=-1, keepdims=True)).astype(o_ref.dtype)


def _context_attention(pc):
    b, lc, _ = pc.shape
    spec = lambda col: pl.BlockSpec((1, lc, B_WIDTH), lambda bi: (bi, 0, col))
    return pl.pallas_call(
        _ctx_attn_kernel,
        grid=(b,),
        in_specs=[spec(2), spec(3), spec(4)],
        out_specs=pl.BlockSpec((1, lc, B_WIDTH), lambda bi: (bi, 0, 0)),
        out_shape=jax.ShapeDtypeStruct((b, lc, B_WIDTH), BF16),
        compiler_params=_params(("arbitrary",)),
        name="context_attention",
    )(pc, pc, pc)


def _store_residual_and_route(xf_ref, gam_ref, sh_ref, sc_ref, wr_ref, o_ref, h_ref, aff_ref):
    tm = xf_ref.shape[0]
    xn = xf_ref[...]
    h = _norm_mod(xn, gam_ref[...], sh_ref[0], sc_ref[0])
    _slab_zero_pad(o_ref, (0,), tm)
    _slab_zero_pad(h_ref, (0,), tm)
    for c in range(SLAB_ROWS):
        sl = slice(c * LANES, (c + 1) * LANES)
        o_ref[_slab_cols(o_ref, (0,), c, tm)] = xn[:, sl]
        h_ref[_slab_cols(h_ref, (0,), c, tm)] = h[:, sl]
    ne = aff_ref.shape[1]
    h_hi = h.astype(BF16)
    h_lo = (h - h_hi.astype(F32)).astype(BF16)
    p_hi = jnp.dot(h_hi, wr_ref[...], preferred_element_type=F32).T
    p_lo = jnp.dot(h_lo, wr_ref[...], preferred_element_type=F32).T
    logits = p_hi[0:ne] + (p_hi[ne:2 * ne] + p_lo[0:ne])
    m = jnp.max(logits, axis=0, keepdims=True)
    e = jnp.exp(logits - m)
    aff_ref[0] = e / jnp.sum(e, axis=0, keepdims=True)


def _router_weight(router_w):
    d, ne = router_w.shape
    w_hi = router_w.astype(BF16)
    w_lo = (router_w - w_hi.astype(F32)).astype(BF16)
    return jnp.concatenate([w_hi, w_lo, jnp.zeros((d, LANES - 2 * ne), BF16)], axis=1)


def _route_specs(b, n, d, ne, tm):
    slab = pl.BlockSpec((1, tm * PITCH, LANES), lambda bi, i: (bi, i, 0))
    mod = pl.BlockSpec((1, 1, d), lambda bi, i: (bi, 0, 0))
    in_specs = [_resident((1, d)), mod, mod, _resident((d, LANES))]
    out_specs = [slab, slab, pl.BlockSpec((1, ne, tm), lambda bi, i: (bi, 0, i))]
    slab_shape = jax.ShapeDtypeStruct((b, n * PITCH, LANES), F32)
    return in_specs, out_specs, [slab_shape, slab_shape, jax.ShapeDtypeStruct((b, ne, n), F32)]


def _ab_out_kernel(u_ref, v_ref, ws_ref, bs_ref, b_ref, w_ref, x_ref, g_ref, gam_ref, sh_ref, sc_ref, wr_ref,
                   o_ref, h_ref, aff_ref, xf_ref, a_ref, *, tn):
    _gmlp_tile(u_ref, v_ref, ws_ref, bs_ref, a_ref)
    ka = a_ref.shape[1]
    for jt in range(w_ref.shape[1] // tn):
        cols = slice(jt * tn, (jt + 1) * tn)
        acc = jnp.dot(a_ref[...], w_ref[0:ka, cols], preferred_element_type=F32)
        acc += jnp.dot(b_ref[0], w_ref[ka:2 * ka, cols], preferred_element_type=F32)
        xf_ref[:, cols] = x_ref[0, :, cols] + g_ref[0, :, cols] * acc
    _store_residual_and_route(xf_ref, gam_ref, sh_ref, sc_ref, wr_ref, o_ref, h_ref, aff_ref)


def _ab_out(p, ws, bs_b, bm, w, x, gate, gamma2, shift2, scale2, wr_split, tm, tn=512):
    b, n, d = x.shape
    rin, rout, rshape = _route_specs(b, n, d, N_EXPERTS, tm)
    return pl.pallas_call(
        functools.partial(_ab_out_kernel, tn=tn),
        grid=(b, n // tm),
        in_specs=[
            pl.BlockSpec((1, tm, A_WIDTH), lambda bi, i: (bi, i, 0)),
            pl.BlockSpec((1, tm, A_WIDTH), lambda bi, i: (bi, i, 1)),
            _resident(ws.shape),
            _resident(bs_b.shape),
            pl.BlockSpec((1, tm, B_WIDTH), lambda bi, i: (bi, i, 0)),
            _resident(w.shape),
            pl.BlockSpec((1, tm, d), lambda bi, i: (bi, i, 0)),
            pl.BlockSpec((1, 1, d), lambda bi, i: (bi, 0, 0)),
            *rin,
        ],
        out_specs=rout,
        out_shape=rshape,
        scratch_shapes=[pltpu.VMEM((tm, d), F32), pltpu.VMEM((tm, A_WIDTH), BF16)],
        compiler_params=_params(("arbitrary", "arbitrary")),
        name="ab_out_proj",
    )(p, p, ws, bs_b, bm, w, x, gate, gamma2, shift2, scale2, wr_split)


def _c_in_kernel(x_ref, gam_ref, sh_ref, sc_ref, w_ref, bg_ref, cz_ref, h_ref, xf_ref, *, tn):
    _slab_load_rows(x_ref, (0,), h_ref.shape[0], xf_ref)
    h_ref[...] = _norm_mod(xf_ref[...], gam_ref[...], sh_ref[0], sc_ref[0]).astype(BF16)
    cw = w_ref.shape[1] // 3
    for jt in range(cw // tn):
        cols = slice(jt * tn, (jt + 1) * tn)
        proj = lambda seg: jnp.dot(h_ref[...], w_ref[:, seg * cw + jt * tn:seg * cw + (jt + 1) * tn],
                                   preferred_element_type=F32)
        bg_ref[0, :, cols] = proj(0).astype(bg_ref.dtype)
        cz_ref[0, :, cols] = (proj(1) * proj(2)).astype(cz_ref.dtype)


def _c_in(x, gamma, shift, scale, w, tm, tn=512):
    b = x.shape[0]
    n = x.shape[1] // PITCH
    d = w.shape[0]
    cw = w.shape[1] // 3
    out = jax.ShapeDtypeStruct((b, n, cw), BF16)
    ospec = pl.BlockSpec((1, tm, cw), lambda bi, i: (bi, i, 0))
    return pl.pallas_call(
        functools.partial(_c_in_kernel, tn=tn),
        grid=(b, n // tm),
        in_specs=[
            pl.BlockSpec((1, tm * PITCH, LANES), lambda bi, i: (bi, i, 0)),
            _resident((1, d)),
            pl.BlockSpec((1, 1, d), lambda bi, i: (bi, 0, 0)),
            pl.BlockSpec((1, 1, d), lambda bi, i: (bi, 0, 0)),
            _resident(w.shape),
        ],
        out_specs=[ospec, ospec],
        out_shape=[out, out],
        scratch_shapes=[pltpu.VMEM((tm, d), BF16), pltpu.VMEM((tm, d), F32)],
        compiler_params=_params(("arbitrary", "arbitrary")),
        name="c_in_proj",
    )(x, gamma, shift, scale, w)


HALO = 8


def _c_out_kernel(bg_ref, cz_ref, czp_ref, czn_ref, cw_ref, w_ref, x_ref, g_ref, gam_ref, sh_ref, sc_ref, wr_ref,
                  o_ref, h_ref, aff_ref, xf_ref, acc_ref, *, kchunk, tn):
    i = pl.program_id(1)
    ni = pl.num_programs(1)
    tm = cz_ref.shape[1]
    cw = cz_ref.shape[2]
    _slab_load_rows(x_ref, (0,), tm, xf_ref)
    row = lax.broadcasted_iota(jnp.int32, (tm, kchunk), 0)
    for kc in range(cw // kchunk):
        ks = slice(kc * kchunk, (kc + 1) * kchunk)
        cz = cz_ref[0, :, ks].astype(F32)
        prev_row = jnp.where(i > 0, czp_ref[0, HALO - 1:HALO, ks].astype(F32), 0.0)
        next_row = jnp.where(i < ni - 1, czn_ref[0, 0:1, ks].astype(F32), 0.0)
        up = jnp.where(row == 0, prev_row, pltpu.roll(cz, 1, 0))
        dn = jnp.where(row == tm - 1, next_row, pltpu.roll(cz, tm - 1, 0))
        y = cw_ref[0:1, ks] * up + cw_ref[1:2, ks] * cz + cw_ref[2:3, ks] * dn
        lhs = (bg_ref[0, :, ks].astype(F32) * y).astype(BF16)
        for jt in range(w_ref.shape[1] // tn):
            cols = slice(jt * tn, (jt + 1) * tn)
            part = jnp.dot(lhs, w_ref[ks, cols], preferred_element_type=F32)
            if kc == 0:
                acc_ref[:, cols] = part
            else:
                acc_ref[:, cols] += part
    xf_ref[...] = xf_ref[...] + g_ref[0] * acc_ref[...]
    _store_residual_and_route(xf_ref, gam_ref, sh_ref, sc_ref, wr_ref, o_ref, h_ref, aff_ref)


def _c_out(bg, cz, conv_w, w, x, gate, gamma2, shift2, scale2, wr_split, tm, tn=512):
    b, n, cw = bg.shape
    d = w.shape[1]
    hb = tm // HALO
    nh = n // HALO
    rin, rout, rshape = _route_specs(b, n, d, N_EXPERTS, tm)
    return pl.pallas_call(
        functools.partial(_c_out_kernel, kchunk=512, tn=tn),
        grid=(b, n // tm),
        in_specs=[
            pl.BlockSpec((1, tm, cw), lambda bi, i: (bi, i, 0)),
            pl.BlockSpec((1, tm, cw), lambda bi, i: (bi, i, 0)),
            pl.BlockSpec((1, HALO, cw), lambda bi, i: (bi, jnp.maximum(i * hb - 1, 0), 0)),
            pl.BlockSpec((1, HALO, cw), lambda bi, i: (bi, jnp.minimum((i + 1) * hb, nh - 1), 0)),
            _resident((CONV_W, cw)),
            _resident(w.shape),
            pl.BlockSpec((1, tm * PITCH, LANES), lambda bi, i: (bi, i, 0)),
            pl.BlockSpec((1, 1, d), lambda bi, i: (bi, 0, 0)),
            *rin,
        ],
        out_specs=rout,
        out_shape=rshape,
        scratch_shapes=[pltpu.VMEM((tm, d), F32), pltpu.VMEM((tm, d), F32)],
        compiler_params=_params(("arbitrary", "arbitrary")),
        name="c_out_proj",
    )(bg, cz, cz, cz, conv_w, w, x, gate, gamma2, shift2, scale2, wr_split)


SEARCH_BITS = 3
SELECT_GROUP = 4


def _select_kernel(aff_ref, idx_ref, gate_ref, *, cap, cchunk):
    group = aff_ref.shape[1]
    vs = [aff_ref[0, g] for g in range(group)]

    def count(mask):
        return jnp.sum(jnp.sum(mask.astype(F32), axis=1, keepdims=True), axis=0, keepdims=True)

    def search(it, ts):
        shift = 30 - SEARCH_BITS * (it + 1)
        out = []
        for v, t in zip(vs, ts):
            digit = jnp.zeros((1, 1), jnp.int32)
            for j in range(1, 2 ** SEARCH_BITS):
                cand = pltpu.bitcast(t | (jnp.int32(j) << shift), F32)
                digit += (count(v >= cand) >= cap).astype(jnp.int32)
            out.append(t | (digit << shift))
        return tuple(out)

    thrs = lax.fori_loop(0, 30 // SEARCH_BITS, search, tuple(jnp.zeros((1, 1), jnp.int32) for _ in vs))
    for g, (v, thr_bits) in enumerate(zip(vs, thrs)):
        _compact_selected(v, pltpu.bitcast(thr_bits, F32), idx_ref.at[0, g], gate_ref.at[0, g], cap, cchunk, count)


def _compact_selected(v, thr, idx_ref, gate_ref, cap, cchunk, count):
    nr = v.shape[0]
    gt = v > thr
    eq = v == thr
    need = cap - count(gt)

    lane_l = lax.broadcasted_iota(jnp.int32, (LANES, LANES), 0)
    lane_c = lax.broadcasted_iota(jnp.int32, (LANES, LANES), 1)
    tri_lane = (lane_l <= lane_c).astype(BF16)
    row_r = lax.broadcasted_iota(jnp.int32, (nr, nr), 0)
    row_c = lax.broadcasted_iota(jnp.int32, (nr, nr), 1)
    tri_row = (row_c <= row_r).astype(BF16)

    def prefix(mask):
        mf = mask.astype(BF16)
        in_row = jnp.dot(mf, tri_lane, preferred_element_type=F32)
        colcum = jnp.dot(tri_row, mf, preferred_element_type=F32)
        row_incl = jnp.sum(colcum, axis=1, keepdims=True)
        row_tot = jnp.sum(mask.astype(F32), axis=1, keepdims=True)
        return in_row, row_incl - row_tot, row_incl

    eq_in, eq_off, _ = prefix(eq)
    eq_rank = eq_in + eq_off - eq.astype(F32)
    sel = gt | (eq & (eq_rank < need))
    _, sel_off, sel_incl = prefix(sel)

    self_bf = sel.astype(BF16)
    tri_lane_t = (lane_c <= lane_l).astype(BF16)
    dn_t = (((1,), (1,)), ((), ()))
    pt = lax.dot_general(tri_lane_t, self_bf, dn_t, preferred_element_type=F32)
    eye = (lane_l == lane_c).astype(F32)
    vt = lax.dot_general(eye, v, dn_t, precision=HIGHEST, preferred_element_type=F32)

    for c0 in range(0, cap, cchunk):
        cc = min(cchunk, cap - c0)
        slot = (lax.broadcasted_iota(jnp.int32, (1, cc), 1) + c0).astype(F32)
        r_of = jnp.sum((sel_incl <= slot).astype(F32), axis=0, keepdims=True)
        onehot = (lax.broadcasted_iota(jnp.int32, (nr, cc), 0).astype(F32) == r_of)
        onehot_f = onehot.astype(F32)
        local = slot - jnp.sum(onehot_f * sel_off, axis=0, keepdims=True)
        prow = jnp.dot(pt.astype(BF16), onehot.astype(BF16), preferred_element_type=F32)
        l_of = jnp.sum((prow <= local).astype(F32), axis=0, keepdims=True)
        vrow = jnp.dot(vt, onehot_f, precision=HIGHEST, preferred_element_type=F32)
        lane_i = lax.broadcasted_iota(jnp.int32, (LANES, cc), 0).astype(F32)
        gsel = jnp.sum(jnp.where(lane_i == l_of, vrow, 0.0), axis=0, keepdims=True)
        idx_ref[:, c0:c0 + cc] = (r_of * LANES + l_of).astype(jnp.int32)
        gate_ref[:, c0:c0 + cc] = gsel


MIN_SELECT_ROWS = 8


def _select(aff_t, cap):
    b, ne, n = aff_t.shape
    if n < MIN_SELECT_ROWS * LANES:
        assert cap <= n
        aff_t = jnp.pad(aff_t, ((0, 0), (0, 0), (0, MIN_SELECT_ROWS * LANES - n)), constant_values=-1.0)
        n = MIN_SELECT_ROWS * LANES
    nr = n // LANES
    group = SELECT_GROUP
    assert ne % group == 0
    out = lambda dt: jax.ShapeDtypeStruct((b, ne, 1, cap), dt)
    ospec = pl.BlockSpec((1, group, 1, cap), lambda bi, e: (bi, e, 0, 0))
    idx, gate = pl.pallas_call(
        functools.partial(_select_kernel, cap=cap, cchunk=512),
        grid=(b, ne // group),
        in_specs=[pl.BlockSpec((1, group, nr, LANES), lambda bi, e: (bi, e, 0, 0))],
        out_specs=[ospec, ospec],
        out_shape=[out(jnp.int32), out(F32)],
        compiler_params=_params(("arbitrary", "arbitrary")),
        name="moe_select",
    )(aff_t.reshape(b, ne, nr, LANES))
    return idx.reshape(b, ne, cap), gate.reshape(b, ne, cap)


X_SLOTS = 2
O_SLOTS = 3
ISSUE_UNROLL = 8


def _moe_kernel(idx_ref, idxn_ref, g_ref, g2_ref, h_hbm, wg_hbm, wu_hbm, wd_hbm, x_hbm, o_hbm,
                xg, og, xs_ref, wg_buf, wu_buf, wd_buf, wg_stage, wu_stage, wd_stage, sem_x, sem_o, sem_s, sem_w,
                *, n_tok, layer, nb, nt):
    del x_hbm
    e, bi, t = pl.program_id(0), pl.program_id(1), pl.program_id(2)
    ne = pl.num_programs(0)
    step = (e * nb + bi) * nt + t
    last = ne * nb * nt - 1
    tc = xs_ref.shape[0]
    moved = tc * SLAB_ROWS

    chunks = nb * nt
    chunk = bi * nt + t
    wslot = e % 2
    stages = (wg_stage, wu_stage, wd_stage)
    bufs = (wg_buf, wu_buf, wd_buf)

    def weight_rows(ref, k):
        rows = ref.shape[0]
        return pl.ds(pl.multiple_of(k * rows, rows), rows)

    def weight_copies(expert, k):
        return [pltpu.make_async_copy(hbm.at[layer, expert, weight_rows(stage, k), :], stage, sem_w.at[i])
                for i, (hbm, stage) in enumerate(zip((wg_hbm, wu_hbm, wd_hbm), stages))]

    def cast_chunk(slot, k):
        for stage, buf in zip(stages, bufs):
            buf[slot, weight_rows(stage, k), :] = stage[...].astype(BF16)

    @pl.when(step == 0)
    def _():
        def load(k, carry):
            copies = weight_copies(0, k)
            for cp in copies:
                cp.start()
            for cp in copies:
                cp.wait()
            cast_chunk(0, k)
            return carry
        lax.fori_loop(0, chunks, load, 0)

    @pl.when(e + 1 < ne)
    def _():
        for cp in weight_copies(e + 1, chunk):
            cp.start()

    def for_each_slot_row(idx, sample, fn):
        def body(s8, carry):
            for u in range(ISSUE_UNROLL):
                s = s8 * ISSUE_UNROLL + u
                src = pl.multiple_of((sample * n_tok + idx[0, 0, s]) * PITCH, 4)
                fn(pl.ds(src, SLAB_ROWS), pl.ds(pl.multiple_of(s * PITCH, 4), SLAB_ROWS))
            return carry
        lax.fori_loop(0, tc // ISSUE_UNROLL, body, 0)

    def issue_gathers(idx, sample, xslot, oslot):
        def one(src, dst):
            pltpu.make_async_copy(h_hbm.at[src, :], xg.at[xslot, dst, :], sem_x.at[xslot]).start()
            pltpu.make_async_copy(o_hbm.at[src, :], og.at[oslot, dst, :], sem_o.at[oslot]).start()
        for_each_slot_row(idx, sample, one)

    def wait_rows(hbm, buf, slot, sem, to_hbm):
        a, b = hbm.at[pl.ds(0, moved), :], buf.at[slot, pl.ds(0, moved), :]
        (pltpu.make_async_copy(b, a, sem.at[slot]) if to_hbm else pltpu.make_async_copy(a, b, sem.at[slot])).wait()

    @pl.when(step == 0)
    def _():
        issue_gathers(idx_ref, bi, 0, 0)

    @pl.when(step >= 2)
    def _():
        wait_rows(o_hbm, og, (step - 2) % O_SLOTS, sem_s, True)

    @pl.when(step < last)
    def _():
        nxt = step + 1
        issue_gathers(idxn_ref, (nxt // nt) % nb, nxt % X_SLOTS, nxt % O_SLOTS)

    xslot = step % X_SLOTS
    oslot = step % O_SLOTS
    wait_rows(h_hbm, xg, xslot, sem_x, False)
    for c in range(SLAB_ROWS):
        xs_ref[:, c * LANES:(c + 1) * LANES] = xg[xslot, pl.ds(c, tc, stride=PITCH), :].astype(BF16)
    xs = xs_ref[...]
    gate = jnp.dot(xs, wg_buf[wslot], preferred_element_type=F32)
    up = jnp.dot(xs, wu_buf[wslot], preferred_element_type=F32)
    hid = (_silu(gate) * up).astype(BF16)
    y = jnp.dot(hid, wd_buf[wslot], preferred_element_type=F32)
    eye = lax.broadcasted_iota(jnp.int32, (tc, tc), 0) == lax.broadcasted_iota(jnp.int32, (tc, tc), 1)
    gcol = jnp.sum(jnp.where(eye, g_ref[0], 0.0), axis=1, keepdims=True)
    y = y * gcol

    wait_rows(o_hbm, og, oslot, sem_o, False)
    for c in range(SLAB_ROWS):
        cols = slice(c * LANES, (c + 1) * LANES)
        rows = (oslot, pl.ds(c, tc, stride=PITCH), slice(None))
        og[rows] = og[rows] + g2_ref[0][:, cols] * y[:, cols]

    def scatter(src, dst):
        pltpu.make_async_copy(og.at[oslot, dst, :], o_hbm.at[src, :], sem_s.at[oslot]).start()
    for_each_slot_row(idx_ref, bi, scatter)

    @pl.when(e + 1 < ne)
    def _():
        for cp in weight_copies(e + 1, chunk):
            cp.wait()
        cast_chunk(1 - wslot, chunk)

    @pl.when(step == last)
    def _():
        @pl.when(step >= 1)
        def _():
            wait_rows(o_hbm, og, (step - 1) % O_SLOTS, sem_s, True)
        wait_rows(o_hbm, og, oslot, sem_s, True)


def _moe_experts(idx, gate, h, wg, wu, wd, layer, gate2, x, tc):
    b, ne, cap = idx.shape
    n = x.shape[1] // PITCH
    d, f = wg.shape[2], wg.shape[3]
    nt = cap // tc
    assert b >= 2 and nt >= 2 and tc % ISSUE_UNROLL == 0 and d == SLAB_ROWS * LANES
    nsteps = ne * b * nt

    def cur(e, bi, t):
        return ((bi * ne + e) * nt + t, 0, 0)

    def nxt(e, bi, t):
        step = jnp.minimum((e * b + bi) * nt + t + 1, nsteps - 1)
        return (((step // nt) % b * ne + step // (nt * b)) * nt + step % nt, 0, 0)

    idx3 = idx.reshape(b * ne * nt, 1, tc)
    rows = (tc * PITCH, LANES)
    chunks = b * nt
    assert d % (16 * chunks) == 0 and f % (16 * chunks) == 0
    hbm = pl.BlockSpec(memory_space=pl.ANY)
    out = pl.pallas_call(
        functools.partial(_moe_kernel, n_tok=n, layer=layer, nb=b, nt=nt),
        grid=(ne, b, nt),
        in_specs=[
            pl.BlockSpec((1, 1, tc), cur, memory_space=pltpu.SMEM),
            pl.BlockSpec((1, 1, tc), nxt, memory_space=pltpu.SMEM),
            pl.BlockSpec((1, 1, tc), cur),
            pl.BlockSpec((1, 1, d), lambda e, bi, t: (bi, 0, 0)),
            hbm, hbm, hbm, hbm, hbm,
        ],
        out_specs=hbm,
        out_shape=jax.ShapeDtypeStruct((b * n * PITCH, LANES), F32),
        input_output_aliases={8: 0},
        scratch_shapes=[
            pltpu.VMEM((X_SLOTS, *rows), F32), pltpu.VMEM((O_SLOTS, *rows), F32), pltpu.VMEM((tc, d), BF16),
            pltpu.VMEM((2, d, f), BF16), pltpu.VMEM((2, d, f), BF16), pltpu.VMEM((2, f, d), BF16),
            pltpu.VMEM((d // chunks, f), F32), pltpu.VMEM((d // chunks, f), F32), pltpu.VMEM((f // chunks, d), F32),
            pltpu.SemaphoreType.DMA((X_SLOTS,)), pltpu.SemaphoreType.DMA((O_SLOTS,)),
            pltpu.SemaphoreType.DMA((O_SLOTS,)), pltpu.SemaphoreType.DMA((3,)),
        ],
        compiler_params=_params(("arbitrary", "arbitrary", "arbitrary"), disable_bounds_checks=True),
        name="moe_experts",
    )(idx3, idx3, gate.reshape(b * ne * nt, 1, tc), gate2, h.reshape(b * n * PITCH, LANES), wg, wu, wd,
      x.reshape(b * n * PITCH, LANES))
    return out.reshape(x.shape)


def _ec_moe_residual(x, h, aff_t, gate2, wg, wu, wd, layer, *, tc):
    n = x.shape[1] // PITCH
    cap = CAP_FACTOR * n // N_EXPERTS
    idx, g = _select(aff_t, cap)
    return _moe_experts(idx, g, h, wg, wu, wd, layer, gate2, x, tc)


def _slab_to_std_kernel(x_ref, o_ref):
    _slab_load_rows(x_ref, (0,), o_ref.shape[1], o_ref.at[0])


def _slab_to_std(x, d, tm):
    b = x.shape[0]
    n = x.shape[1] // PITCH
    return pl.pallas_call(
        _slab_to_std_kernel,
        grid=(b, n // tm),
        in_specs=[pl.BlockSpec((1, tm * PITCH, LANES), lambda bi, i: (bi, i, 0))],
        out_specs=pl.BlockSpec((1, tm, d), lambda bi, i: (bi, i, 0)),
        out_shape=jax.ShapeDtypeStruct((b, n, d), F32),
        compiler_params=_params(("arbitrary", "arbitrary")),
        name="slab_to_std",
    )(x)


CTX_MOE_TILE = 16


def kernel(x, c, ctx, c_ctx, ada_w, ada_b, norm1_g, norm2_g, ab_w_in, ab_w_out, a_ws, a_bs, a_vnorm_g,
           b_qnorm_g, b_knorm_g, b_rpb, c_w_in, c_conv_w, c_w_out, router_w, moe_w_gate, moe_w_up, moe_w_down):
    bsz, seq, d = x.shape
    lc = ctx.shape[1]
    depth = ada_w.shape[0]
    assert depth == 2, "layer plan: mixer A/B layer (updates the context stream) then mixer C layer"

    cond = jnp.concatenate([c, c_ctx[None], jnp.zeros((8 - bsz - 1, d), F32)], axis=0)
    mod = _ada(cond, ada_w, ada_b)

    wg_all, wu_all, wd_all = moe_w_gate, moe_w_up, moe_w_down
    xl = x
    xc = ctx
    for i in range(depth):
        upd_ctx = i < depth - 1
        sh1, sc1, g1, sh2, sc2, g2 = [mod[i, :bsz, k * d:(k + 1) * d].reshape(bsz, 1, d) for k in range(6)]
        csh1, csc1, cg1, csh2, csc2, cg2 = [
            jnp.broadcast_to(mod[i, bsz, k * d:(k + 1) * d].reshape(1, 1, d), (bsz, 1, d)) for k in range(6)]
        gam1 = norm1_g[i].reshape(1, d)
        gam2 = norm2_g[i].reshape(1, d)
        wr_split = _router_weight(router_w[i])
        j = i // 2
        if i % 2 == 0:
            w_in = ab_w_in[j].astype(BF16)
            w_out = ab_w_out[j].astype(BF16)
            ones = jnp.ones((A_WIDTH,), F32)
            gain = jnp.concatenate([
                ones, a_vnorm_g[j].reshape(-1),
                jnp.tile(b_qnorm_g[j], B_HEADS) * (B_DIM ** -0.5),
                jnp.tile(b_knorm_g[j], B_HEADS), ones]).reshape(1, -1)
            p = _ab_in(xl, gam1, sh1, sc1, w_in, gain, tm=512)
            pc = _ab_in(xc, gam1, csh1, csc1, w_in, gain, tm=lc)
            ws = a_ws[j].astype(BF16)
            bs_b = jnp.broadcast_to(a_bs[j][:, :, None], (A_GROUPS, CHUNK, A_DIM)).astype(F32)
            b_l = _neighborhood_attention(p, pc, _na_bias_table(b_rpb[j], seq // GRID_W))
            xl, h2, aff_t = _ab_out(p, ws, bs_b, b_l, w_out, xl, g1, gam2, sh2, sc2, wr_split, tm=512)
            if upd_ctx:
                b_c = _context_attention(pc)
                xc, hc2, affc_t = _ab_out(pc, ws, bs_b, b_c, w_out, xc, cg1, gam2, csh2, csc2, wr_split, tm=lc)
        else:
            assert not upd_ctx
            bg, cz = _c_in(xl, gam1, sh1, sc1, c_w_in[j].astype(BF16), tm=512)
            xl, h2, aff_t = _c_out(bg, cz, c_conv_w[j], c_w_out[j].astype(BF16), xl, g1, gam2, sh2, sc2,
                                   wr_split, tm=512)

        xl = _ec_moe_residual(xl, h2, aff_t, g2, wg_all, wu_all, wd_all, i, tc=256)
        if upd_ctx:
            xc = _ec_moe_residual(xc, hc2, affc_t, cg2, wg_all, wu_all, wd_all, i, tc=CTX_MOE_TILE)
            xc = _slab_to_std(xc, d, tm=lc)
    return _slab_to_std(xl, d, tm=512)
```

```python
import functools
import math

import jax
import jax.numpy as jnp
import numpy as np
from jax import lax
from jax.experimental import pallas as pl
from jax.experimental.pallas import tpu as pltpu

GRID_W = 64
CHUNK = 128
A_GROUPS = 8
A_DIM = 128
A_WIDTH = A_GROUPS * A_DIM
B_HEADS = 8
B_DIM = 128
B_WIDTH = B_HEADS * B_DIM
NA_ROWS = 8
NA_COLS = 16
CONV_W = 3
N_EXPERTS = 16
CAP_FACTOR = 2
EPS = 1e-6

LANES = 128
MXU_COLS = 256
VMEM_LIMIT = 56 * 1024 * 1024

F32 = jnp.float32
BF16 = jnp.bfloat16
HIGHEST = lax.Precision.HIGHEST
MASK_VALUE = -1e30


SLAB_ROWS = 16
PITCH = 20


def _params(sem, **kw):
    return pltpu.CompilerParams(dimension_semantics=sem, vmem_limit_bytes=VMEM_LIMIT, **kw)


def _slab_cols(ref, lead, c, tm):
    return (*lead, pl.ds(c, tm, stride=PITCH), slice(None))


def _slab_zero_pad(ref, lead, tm):
    for c in range(SLAB_ROWS, PITCH):
        ref[_slab_cols(ref, lead, c, tm)] = jnp.zeros((tm, LANES), ref.dtype)


def _slab_load_rows(ref, lead, tm, dst_ref):
    for c in range(SLAB_ROWS):
        dst_ref[:, c * LANES:(c + 1) * LANES] = ref[_slab_cols(ref, lead, c, tm)]


def _silu(x):
    return x * (1.0 / (1.0 + jnp.exp(-x)))


def _gelu_tanh(x):
    return 0.5 * x * (1.0 + jnp.tanh(math.sqrt(2.0 / math.pi) * (x + 0.044715 * (x * x * x))))


def _norm_mod(x, gamma, shift, scale):
    ms = jnp.mean(x * x, axis=-1, keepdims=True)
    return (x * lax.rsqrt(ms + EPS) * gamma) * (1.0 + scale) + shift


def _ada_kernel(c_ref, w_ref, b_ref, o_ref):
    s = _silu(c_ref[...])
    o_ref[0] = jnp.dot(s, w_ref[0], precision=HIGHEST, preferred_element_type=F32) + b_ref[0]


def _ada(cond, ada_w, ada_b):
    depth, d, n6 = ada_w.shape
    tn = 1024
    return pl.pallas_call(
        _ada_kernel,
        grid=(depth, n6 // tn),
        in_specs=[
            pl.BlockSpec((8, d), lambda l, j: (0, 0)),
            pl.BlockSpec((1, d, tn), lambda l, j: (l, 0, j)),
            pl.BlockSpec((1, 1, tn), lambda l, j: (l, 0, j)),
        ],
        out_specs=pl.BlockSpec((1, 8, tn), lambda l, j: (l, 0, j)),
        out_shape=jax.ShapeDtypeStruct((depth, 8, n6), F32),
        compiler_params=_params(("arbitrary", "arbitrary")),
        name="ada_mod",
    )(cond, ada_w, ada_b.reshape(depth, 1, n6))


AB_SEGMENT_EPILOGUES = ("gelu", "gelu_norm", "norm", "norm", "none")


def _ab_in_kernel(x_ref, gam_ref, sh_ref, sc_ref, w_ref, gain_ref, o_ref, h_ref, *, tn):
    h_ref[...] = _norm_mod(x_ref[0], gam_ref[...], sh_ref[0], sc_ref[0]).astype(BF16)
    per_seg = A_WIDTH // tn
    for jt in range(w_ref.shape[1] // tn):
        kind = AB_SEGMENT_EPILOGUES[jt // per_seg]
        acc = jnp.dot(h_ref[...], w_ref[:, jt * tn:(jt + 1) * tn], preferred_element_type=F32)
        if kind.startswith("gelu"):
            acc = _gelu_tanh(acc)
        if kind.endswith("norm"):
            for g in range(tn // LANES):
                sl = slice(jt * tn + g * LANES, jt * tn + (g + 1) * LANES)
                ag = acc[:, g * LANES:(g + 1) * LANES]
                ms = jnp.mean(ag * ag, axis=-1, keepdims=True)
                o_ref[0, :, sl] = (ag * lax.rsqrt(ms + EPS) * gain_ref[:, sl]).astype(o_ref.dtype)
        else:
            o_ref[0, :, jt * tn:(jt + 1) * tn] = acc.astype(o_ref.dtype)


def _resident(shape):
    nd = len(shape)
    return pl.BlockSpec(shape, lambda *_: (0,) * nd, pipeline_mode=pl.Buffered(1))


def _ab_in(x, gamma, shift, scale, w, gain, tm, tn=512):
    b, n, d = x.shape
    f = w.shape[1]
    return pl.pallas_call(
        functools.partial(_ab_in_kernel, tn=tn),
        grid=(b, n // tm),
        in_specs=[
            pl.BlockSpec((1, tm, d), lambda bi, i: (bi, i, 0)),
            _resident((1, d)),
            pl.BlockSpec((1, 1, d), lambda bi, i: (bi, 0, 0)),
            pl.BlockSpec((1, 1, d), lambda bi, i: (bi, 0, 0)),
            _resident((d, f)),
            _resident((1, f)),
        ],
        out_specs=pl.BlockSpec((1, tm, f), lambda bi, i: (bi, i, 0)),
        out_shape=jax.ShapeDtypeStruct((b, n, f), BF16),
        scratch_shapes=[pltpu.VMEM((tm, d), BF16)],
        compiler_params=_params(("arbitrary", "arbitrary")),
        name="ab_in_proj",
    )(x, gamma, shift, scale, w, gain)


def _gmlp_tile(u_ref, v_ref, ws_ref, bs_ref, a_ref):
    for ch in range(a_ref.shape[0] // CHUNK):
        rows = slice(ch * CHUNK, (ch + 1) * CHUNK)
        for g in range(A_GROUPS):
            cols = slice(g * A_DIM, (g + 1) * A_DIM)
            s = jnp.dot(ws_ref[g], v_ref[0, rows, cols], preferred_element_type=F32) + bs_ref[g]
            a_ref[rows, cols] = (u_ref[0, rows, cols].astype(F32) * s).astype(a_ref.dtype)


Q_ROWS = 4
KV_BLOCKS = 3
MASKED_PLANE = 2 * NA_ROWS - 1


def _na_bias_table(rpb, rows):
    h = rpb.shape[0]
    cols = np.arange(GRID_W)
    cstart = np.clip(cols - NA_COLS // 2, 0, GRID_W - NA_COLS)
    valid = (cols[None, :] >= cstart[:, None]) & (cols[None, :] < cstart[:, None] + NA_COLS)
    col_off = np.clip(cols[None, :] - cols[:, None] + (NA_COLS - 1), 0, 2 * NA_COLS - 2)
    planes = jnp.where(valid[None, None], rpb[:, :, col_off], MASK_VALUE)
    planes = jnp.concatenate([planes, jnp.full((h, 1, GRID_W, GRID_W), MASK_VALUE, planes.dtype)], axis=1)
    nblk = rows // Q_ROWS
    assert nblk >= KV_BLOCKS + 1 and NA_ROWS <= (KV_BLOCKS - 1) * Q_ROWS
    sel = np.full((3, Q_ROWS, KV_BLOCKS * Q_ROWS), MASKED_PLANE, np.int32)
    for variant, g in enumerate((0, 1, nblk - 1)):
        first = min(max(g - 1, 0), nblk - KV_BLOCKS)
        for i in range(Q_ROWS):
            r = g * Q_ROWS + i
            rs = min(max(r - NA_ROWS // 2, 0), rows - NA_ROWS)
            for kr in range(KV_BLOCKS * Q_ROWS):
                key_row = first * Q_ROWS + kr
                if rs <= key_row < rs + NA_ROWS:
                    sel[variant, i, kr] = key_row - r + (NA_ROWS - 1)
    t = planes[:, sel]
    t = jnp.transpose(t, (1, 0, 2, 4, 3, 5))
    return t.reshape(3, h, Q_ROWS * GRID_W, KV_BLOCKS * Q_ROWS * GRID_W).astype(F32)


def _na_kernel(q_ref, k0_ref, k1_ref, k2_ref, v0_ref, v1_ref, v2_ref, kx_ref, vx_ref, bias_ref, o_ref):
    dn = (((1,), (1,)), ((), ()))
    tk = k0_ref.shape[1]
    k_refs = (k0_ref, k1_ref, k2_ref)
    v_refs = (v0_ref, v1_ref, v2_ref)
    for h in range(B_HEADS):
        cols = slice(h * B_DIM, (h + 1) * B_DIM)
        q = q_ref[0, :, cols]
        s = [lax.dot_general(q, k_refs[n][0, :, cols], dn, preferred_element_type=F32)
             + bias_ref[0, h, :, n * tk:(n + 1) * tk] for n in range(KV_BLOCKS)]
        s.append(lax.dot_general(q, kx_ref[0, :, cols], dn, preferred_element_type=F32))
        m = functools.reduce(jnp.maximum, [jnp.max(x, axis=-1, keepdims=True) for x in s])
        p = [jnp.exp(x - m) for x in s]
        l = functools.reduce(jnp.add, [jnp.sum(x, axis=-1, keepdims=True) for x in p])
        o = jnp.dot(p[KV_BLOCKS].astype(BF16), vx_ref[0, :, cols], preferred_element_type=F32)
        for n in range(KV_BLOCKS):
            o += jnp.dot(p[n].astype(BF16), v_refs[n][0, :, cols], preferred_element_type=F32)
        o_ref[0, :, cols] = (o / l).astype(o_ref.dtype)


def _neighborhood_attention(p, pc, bias):
    b, s, _ = p.shape
    lc = pc.shape[1]
    tq = Q_ROWS * GRID_W
    nblk = s // tq
    qcol, kcol, vcol = 2, 3, 4
    first = lambda i: jnp.clip(i - 1, 0, nblk - KV_BLOCKS)
    kv = lambda col, n: pl.BlockSpec((1, tq, B_WIDTH), lambda bi, i: (bi, first(i) + n, col))
    variant = lambda i: jnp.where(i == 0, 0, jnp.where(i == nblk - 1, 2, 1))
    return pl.pallas_call(
        _na_kernel,
        grid=(b, nblk),
        in_specs=[
            pl.BlockSpec((1, tq, B_WIDTH), lambda bi, i: (bi, i, qcol)),
            *[kv(kcol, n) for n in range(KV_BLOCKS)],
            *[kv(vcol, n) for n in range(KV_BLOCKS)],
            pl.BlockSpec((1, lc, B_WIDTH), lambda bi, i: (bi, 0, kcol)),
            pl.BlockSpec((1, lc, B_WIDTH), lambda bi, i: (bi, 0, vcol)),
            pl.BlockSpec((1, *bias.shape[1:]), lambda bi, i: (variant(i), 0, 0, 0)),
        ],
        out_specs=pl.BlockSpec((1, tq, B_WIDTH), lambda bi, i: (bi, i, 0)),
        out_shape=jax.ShapeDtypeStruct((b, s, B_WIDTH), BF16),
        compiler_params=_params(("arbitrary", "arbitrary")),
        name="neighborhood_attention",
    )(p, *([p] * (2 * KV_BLOCKS)), pc, pc, bias)


def _ctx_attn_kernel(q_ref, k_ref, v_ref, o_ref):
    for h in range(B_HEADS):
        cols = slice(h * B_DIM, (h + 1) * B_DIM)
        s = lax.dot_general(q_ref[0, :, cols], k_ref[0, :, cols], (((1,), (1,)), ((), ())),
                            preferred_element_type=F32)
        p = jnp.exp(s - jnp.max(s, axis=-1, keepdims=True))
        o = jnp.dot(p.astype(BF16), v_ref[0, :, cols], preferred_element_type=F32)
        o_ref[0, :, cols] = (o / jnp.sum(p, axis=-1, keepdims=True)).astype(o_ref.dtype)


def _context_attention(pc):
    b, lc, _ = pc.shape
    spec = lambda col: pl.BlockSpec((1, lc, B_WIDTH), lambda bi: (bi, 0, col))
    return pl.pallas_call(
        _ctx_attn_kernel,
        grid=(b,),
        in_specs=[spec(2), spec(3), spec(4)],
        out_specs=pl.BlockSpec((1, lc, B_WIDTH), lambda bi: (bi, 0, 0)),
        out_shape=jax.ShapeDtypeStruct((b, lc, B_WIDTH), BF16),
        compiler_params=_params(("arbitrary",)),
        name="context_attention",
    )(pc, pc, pc)


def _store_residual_and_route(xf_ref, gam_ref, sh_ref, sc_ref, wr_ref, o_ref, h_ref, aff_ref):
    tm = xf_ref.shape[0]
    xn = xf_ref[...]
    h = _norm_mod(xn, gam_ref[...], sh_ref[0], sc_ref[0])
    _slab_zero_pad(o_ref, (0,), tm)
    _slab_zero_pad(h_ref, (0,), tm)
    for c in range(SLAB_ROWS):
        sl = slice(c * LANES, (c + 1) * LANES)
        o_ref[_slab_cols(o_ref, (0,), c, tm)] = xn[:, sl]
        h_ref[_slab_cols(h_ref, (0,), c, tm)] = h[:, sl]
    ne = aff_ref.shape[1]
    h_hi = h.astype(BF16)
    h_lo = (h - h_hi.astype(F32)).astype(BF16)
    p_hi = jnp.dot(h_hi, wr_ref[...], preferred_element_type=F32).T
    p_lo = jnp.dot(h_lo, wr_ref[...], preferred_element_type=F32).T
    logits = p_hi[0:ne] + (p_hi[ne:2 * ne] + p_lo[0:ne])
    m = jnp.max(logits, axis=0, keepdims=True)
    e = jnp.exp(logits - m)
    aff_ref[0] = e / jnp.sum(e, axis=0, keepdims=True)


def _router_weight(router_w):
    d, ne = router_w.shape
    w_hi = router_w.astype(BF16)
    w_lo = (router_w - w_hi.astype(F32)).astype(BF16)
    return jnp.concatenate([w_hi, w_lo, jnp.zeros((d, LANES - 2 * ne), BF16)], axis=1)


def _route_specs(b, n, d, ne, tm):
    slab = pl.BlockSpec((1, tm * PITCH, LANES), lambda bi, i: (bi, i, 0))
    mod = pl.BlockSpec((1, 1, d), lambda bi, i: (bi, 0, 0))
    in_specs = [_resident((1, d)), mod, mod, _resident((d, LANES))]
    out_specs = [slab, slab, pl.BlockSpec((1, ne, tm), lambda bi, i: (bi, 0, i))]
    slab_shape = jax.ShapeDtypeStruct((b, n * PITCH, LANES), F32)
    return in_specs, out_specs, [slab_shape, slab_shape, jax.ShapeDtypeStruct((b, ne, n), F32)]


def _ab_out_kernel(u_ref, v_ref, ws_ref, bs_ref, b_ref, w_ref, x_ref, g_ref, gam_ref, sh_ref, sc_ref, wr_ref,
                   o_ref, h_ref, aff_ref, xf_ref, a_ref, *, tn):
    _gmlp_tile(u_ref, v_ref, ws_ref, bs_ref, a_ref)
    ka = a_ref.shape[1]
    for jt in range(w_ref.shape[1] // tn):
        cols = slice(jt * tn, (jt + 1) * tn)
        acc = jnp.dot(a_ref[...], w_ref[0:ka, cols], preferred_element_type=F32)
        acc += jnp.dot(b_ref[0], w_ref[ka:2 * ka, cols], preferred_element_type=F32)
        xf_ref[:, cols] = x_ref[0, :, cols] + g_ref[0, :, cols] * acc
    _store_residual_and_route(xf_ref, gam_ref, sh_ref, sc_ref, wr_ref, o_ref, h_ref, aff_ref)


def _ab_out(p, ws, bs_b, bm, w, x, gate, gamma2, shift2, scale2, wr_split, tm, tn=512):
    b, n, d = x.shape
    rin, rout, rshape = _route_specs(b, n, d, N_EXPERTS, tm)
    return pl.pallas_call(
        functools.partial(_ab_out_kernel, tn=tn),
        grid=(b, n // tm),
        in_specs=[
            pl.BlockSpec((1, tm, A_WIDTH), lambda bi, i: (bi, i, 0)),
            pl.BlockSpec((1, tm, A_WIDTH), lambda bi, i: (bi, i, 1)),
            _resident(ws.shape),
            _resident(bs_b.shape),
            pl.BlockSpec((1, tm, B_WIDTH), lambda bi, i: (bi, i, 0)),
            _resident(w.shape),
            pl.BlockSpec((1, tm, d), lambda bi, i: (bi, i, 0)),
            pl.BlockSpec((1, 1, d), lambda bi, i: (bi, 0, 0)),
            *rin,
        ],
        out_specs=rout,
        out_shape=rshape,
        scratch_shapes=[pltpu.VMEM((tm, d), F32), pltpu.VMEM((tm, A_WIDTH), BF16)],
        compiler_params=_params(("arbitrary", "arbitrary")),
        name="ab_out_proj",
    )(p, p, ws, bs_b, bm, w, x, gate, gamma2, shift2, scale2, wr_split)


def _c_in_kernel(x_ref, gam_ref, sh_ref, sc_ref, w_ref, bg_ref, cz_ref, h_ref, xf_ref, *, tn):
    _slab_load_rows(x_ref, (0,), h_ref.shape[0], xf_ref)
    h_ref[...] = _norm_mod(xf_ref[...], gam_ref[...], sh_ref[0], sc_ref[0]).astype(BF16)
    cw = w_ref.shape[1] // 3
    for jt in range(cw // tn):
        cols = slice(jt * tn, (jt + 1) * tn)
        proj = lambda seg: jnp.dot(h_ref[...], w_ref[:, seg * cw + jt * tn:seg * cw + (jt + 1) * tn],
                                   preferred_element_type=F32)
        bg_ref[0, :, cols] = proj(0).astype(bg_ref.dtype)
        cz_ref[0, :, cols] = (proj(1) * proj(2)).astype(cz_ref.dtype)


def _c_in(x, gamma, shift, scale, w, tm, tn=512):
    b = x.shape[0]
    n = x.shape[1] // PITCH
    d = w.shape[0]
    cw = w.shape[1] // 3
    out = jax.ShapeDtypeStruct((b, n, cw), BF16)
    ospec = pl.BlockSpec((1, tm, cw), lambda bi, i: (bi, i, 0))
    return pl.pallas_call(
        functools.partial(_c_in_kernel, tn=tn),
        grid=(b, n // tm),
        in_specs=[
            pl.BlockSpec((1, tm * PITCH, LANES), lambda bi, i: (bi, i, 0)),
            _resident((1, d)),
            pl.BlockSpec((1, 1, d), lambda bi, i: (bi, 0, 0)),
            pl.BlockSpec((1, 1, d), lambda bi, i: (bi, 0, 0)),
            _resident(w.shape),
        ],
        out_specs=[ospec, ospec],
        out_shape=[out, out],
        scratch_shapes=[pltpu.VMEM((tm, d), BF16), pltpu.VMEM((tm, d), F32)],
        compiler_params=_params(("arbitrary", "arbitrary")),
        name="c_in_proj",
    )(x, gamma, shift, scale, w)


HALO = 8


def _c_out_kernel(bg_ref, cz_ref, czp_ref, czn_ref, cw_ref, w_ref, x_ref, g_ref, gam_ref, sh_ref, sc_ref, wr_ref,
                  o_ref, h_ref, aff_ref, xf_ref, acc_ref, *, kchunk, tn):
    i = pl.program_id(1)
    ni = pl.num_programs(1)
    tm = cz_ref.shape[1]
    cw = cz_ref.shape[2]
    _slab_load_rows(x_ref, (0,), tm, xf_ref)
    row = lax.broadcasted_iota(jnp.int32, (tm, kchunk), 0)
    for kc in range(cw // kchunk):
        ks = slice(kc * kchunk, (kc + 1) * kchunk)
        cz = cz_ref[0, :, ks].astype(F32)
        prev_row = jnp.where(i > 0, czp_ref[0, HALO - 1:HALO, ks].astype(F32), 0.0)
        next_row = jnp.where(i < ni - 1, czn_ref[0, 0:1, ks].astype(F32), 0.0)
        up = jnp.where(row == 0, prev_row, pltpu.roll(cz, 1, 0))
        dn = jnp.where(row == tm - 1, next_row, pltpu.roll(cz, tm - 1, 0))
        y = cw_ref[0:1, ks] * up + cw_ref[1:2, ks] * cz + cw_ref[2:3, ks] * dn
        lhs = (bg_ref[0, :, ks].astype(F32) * y).astype(BF16)
        for jt in range(w_ref.shape[1] // tn):
            cols = slice(jt * tn, (jt + 1) * tn)
            part = jnp.dot(lhs, w_ref[ks, cols], preferred_element_type=F32)
            if kc == 0:
                acc_ref[:, cols] = part
            else:
                acc_ref[:, cols] += part
    xf_ref[...] = xf_ref[...] + g_ref[0] * acc_ref[...]
    _store_residual_and_route(xf_ref, gam_ref, sh_ref, sc_ref, wr_ref, o_ref, h_ref, aff_ref)


def _c_out(bg, cz, conv_w, w, x, gate, gamma2, shift2, scale2, wr_split, tm, tn=512):
    b, n, cw = bg.shape
    d = w.shape[1]
    hb = tm // HALO
    nh = n // HALO
    rin, rout, rshape = _route_specs(b, n, d, N_EXPERTS, tm)
    return pl.pallas_call(
        functools.partial(_c_out_kernel, kchunk=512, tn=tn),
        grid=(b, n // tm),
        in_specs=[
            pl.BlockSpec((1, tm, cw), lambda bi, i: (bi, i, 0)),
            pl.BlockSpec((1, tm, cw), lambda bi, i: (bi, i, 0)),
            pl.BlockSpec((1, HALO, cw), lambda bi, i: (bi, jnp.maximum(i * hb - 1, 0), 0)),
            pl.BlockSpec((1, HALO, cw), lambda bi, i: (bi, jnp.minimum((i + 1) * hb, nh - 1), 0)),
            _resident((CONV_W, cw)),
            _resident(w.shape),
            pl.BlockSpec((1, tm * PITCH, LANES), lambda bi, i: (bi, i, 0)),
            pl.BlockSpec((1, 1, d), lambda bi, i: (bi, 0, 0)),
            *rin,
        ],
        out_specs=rout,
        out_shape=rshape,
        scratch_shapes=[pltpu.VMEM((tm, d), F32), pltpu.VMEM((tm, d), F32)],
        compiler_params=_params(("arbitrary", "arbitrary")),
        name="c_out_proj",
    )(bg, cz, cz, cz, conv_w, w, x, gate, gamma2, shift2, scale2, wr_split)


SEARCH_BITS = 3
SELECT_GROUP = 4


def _select_kernel(aff_ref, idx_ref, gate_ref, *, cap, cchunk):
    group = aff_ref.shape[1]
    vs = [aff_ref[0, g] for g in range(group)]

    def count(mask):
        return jnp.sum(jnp.sum(mask.astype(F32), axis=1, keepdims=True), axis=0, keepdims=True)

    def search(it, ts):
        shift = 30 - SEARCH_BITS * (it + 1)
        out = []
        for v, t in zip(vs, ts):
            digit = jnp.zeros((1, 1), jnp.int32)
            for j in range(1, 2 ** SEARCH_BITS):
                cand = pltpu.bitcast(t | (jnp.int32(j) << shift), F32)
                digit += (count(v >= cand) >= cap).astype(jnp.int32)
            out.append(t | (digit << shift))
        return tuple(out)

    thrs = lax.fori_loop(0, 30 // SEARCH_BITS, search, tuple(jnp.zeros((1, 1), jnp.int32) for _ in vs))
    for g, (v, thr_bits) in enumerate(zip(vs, thrs)):
        _compact_selected(v, pltpu.bitcast(thr_bits, F32), idx_ref.at[0, g], gate_ref.at[0, g], cap, cchunk, count)


def _compact_selected(v, thr, idx_ref, gate_ref, cap, cchunk, count):
    nr = v.shape[0]
    gt = v > thr
    eq = v == thr
    need = cap - count(gt)

    lane_l = lax.broadcasted_iota(jnp.int32, (LANES, LANES), 0)
    lane_c = lax.broadcasted_iota(jnp.int32, (LANES, LANES), 1)
    tri_lane = (lane_l <= lane_c).astype(BF16)
    row_r = lax.broadcasted_iota(jnp.int32, (nr, nr), 0)
    row_c = lax.broadcasted_iota(jnp.int32, (nr, nr), 1)
    tri_row = (row_c <= row_r).astype(BF16)

    def prefix(mask):
        mf = mask.astype(BF16)
        in_row = jnp.dot(mf, tri_lane, preferred_element_type=F32)
        colcum = jnp.dot(tri_row, mf, preferred_element_type=F32)
        row_incl = jnp.sum(colcum, axis=1, keepdims=True)
        row_tot = jnp.sum(mask.astype(F32), axis=1, keepdims=True)
        return in_row, row_incl - row_tot, row_incl

    eq_in, eq_off, _ = prefix(eq)
    eq_rank = eq_in + eq_off - eq.astype(F32)
    sel = gt | (eq & (eq_rank < need))
    _, sel_off, sel_incl = prefix(sel)

    self_bf = sel.astype(BF16)
    tri_lane_t = (lane_c <= lane_l).astype(BF16)
    dn_t = (((1,), (1,)), ((), ()))
    pt = lax.dot_general(tri_lane_t, self_bf, dn_t, preferred_element_type=F32)
    eye = (lane_l == lane_c).astype(F32)
    vt = lax.dot_general(eye, v, dn_t, precision=HIGHEST, preferred_element_type=F32)

    for c0 in range(0, cap, cchunk):
        cc = min(cchunk, cap - c0)
        slot = (lax.broadcasted_iota(jnp.int32, (1, cc), 1) + c0).astype(F32)
        r_of = jnp.sum((sel_incl <= slot).astype(F32), axis=0, keepdims=True)
        onehot = (lax.broadcasted_iota(jnp.int32, (nr, cc), 0).astype(F32) == r_of)
        onehot_f = onehot.astype(F32)
        local = slot - jnp.sum(onehot_f * sel_off, axis=0, keepdims=True)
        prow = jnp.dot(pt.astype(BF16), onehot.astype(BF16), preferred_element_type=F32)
        l_of = jnp.sum((prow <= local).astype(F32), axis=0, keepdims=True)
        vrow = jnp.dot(vt, onehot_f, precision=HIGHEST, preferred_element_type=F32)
        lane_i = lax.broadcasted_iota(jnp.int32, (LANES, cc), 0).astype(F32)
        gsel = jnp.sum(jnp.where(lane_i == l_of, vrow, 0.0), axis=0, keepdims=True)
        idx_ref[:, c0:c0 + cc] = (r_of * LANES + l_of).astype(jnp.int32)
        gate_ref[:, c0:c0 + cc] = gsel


MIN_SELECT_ROWS = 8


def _select(aff_t, cap):
    b, ne, n = aff_t.shape
    if n < MIN_SELECT_ROWS * LANES:
        assert cap <= n
        aff_t = jnp.pad(aff_t, ((0, 0), (0, 0), (0, MIN_SELECT_ROWS * LANES - n)), constant_values=-1.0)
        n = MIN_SELECT_ROWS * LANES
    nr = n // LANES
    group = SELECT_GROUP
    assert ne % group == 0
    out = lambda dt: jax.ShapeDtypeStruct((b, ne, 1, cap), dt)
    ospec = pl.BlockSpec((1, group, 1, cap), lambda bi, e: (bi, e, 0, 0))
    idx, gate = pl.pallas_call(
        functools.partial(_select_kernel, cap=cap, cchunk=512),
        grid=(b, ne // group),
        in_specs=[pl.BlockSpec((1, group, nr, LANES), lambda bi, e: (bi, e, 0, 0))],
        out_specs=[ospec, ospec],
        out_shape=[out(jnp.int32), out(F32)],
        compiler_params=_params(("arbitrary", "arbitrary")),
        name="moe_select",
    )(aff_t.reshape(b, ne, nr, LANES))
    return idx.reshape(b, ne, cap), gate.reshape(b, ne, cap)


X_SLOTS = 2
O_SLOTS = 3
ISSUE_UNROLL = 8


def _moe_kernel(idx_ref, idxn_ref, g_ref, g2_ref, h_hbm, wg_hbm, wu_hbm, wd_hbm, x_hbm, o_hbm,
                xg, og, xs_ref, hid_ref, wg_buf, wu_buf, wd_buf, wg_stage, wu_stage, wd_stage, sem_x, sem_o, sem_s, sem_w,
                *, layer, nb, nt):
    del x_hbm
    e, bi, t = pl.program_id(0), pl.program_id(1), pl.program_id(2)
    ne = pl.num_programs(0)
    step = (e * nb + bi) * nt + t
    last = ne * nb * nt - 1
    tc = xs_ref.shape[0]
    moved = tc * SLAB_ROWS

    chunks = nb * nt
    chunk = bi * nt + t
    wslot = e % 2
    stages = (wg_stage, wu_stage, wd_stage)
    bufs = (wg_buf, wu_buf, wd_buf)

    def weight_rows(ref, k):
        rows = ref.shape[0]
        return pl.ds(pl.multiple_of(k * rows, rows), rows)

    def weight_copies(expert, k):
        return [pltpu.make_async_copy(hbm.at[layer, expert, weight_rows(stage, k), :], stage, sem_w.at[i])
                for i, (hbm, stage) in enumerate(zip((wg_hbm, wu_hbm, wd_hbm), stages))]

    def cast_chunk(slot, k):
        for stage, buf in zip(stages, bufs):
            buf[slot, weight_rows(stage, k), :] = stage[...].astype(BF16)

    @pl.when(step == 0)
    def _():
        def load(k, carry):
            copies = weight_copies(0, k)
            for cp in copies:
                cp.start()
            for cp in copies:
                cp.wait()
            cast_chunk(0, k)
            return carry
        lax.fori_loop(0, chunks, load, 0)

    @pl.when(e + 1 < ne)
    def _():
        for cp in weight_copies(e + 1, chunk):
            cp.start()

    def for_each_slot_row(row_ref, fn):
        def body(s8, carry):
            for u in range(ISSUE_UNROLL):
                s = s8 * ISSUE_UNROLL + u
                fn(pl.ds(pl.multiple_of(row_ref[0, 0, s], 4), SLAB_ROWS),
                   pl.ds(pl.multiple_of(s * PITCH, 4), SLAB_ROWS))
            return carry
        lax.fori_loop(0, tc // ISSUE_UNROLL, body, 0)

    def issue_gathers(row_ref, xslot, oslot):
        def one(src, dst):
            pltpu.make_async_copy(h_hbm.at[src, :], xg.at[xslot, dst, :], sem_x.at[xslot]).start()
            pltpu.make_async_copy(o_hbm.at[src, :], og.at[oslot, dst, :], sem_o.at[oslot]).start()
        for_each_slot_row(row_ref, one)

    def wait_rows(hbm, buf, slot, sem, to_hbm):
        a, b = hbm.at[pl.ds(0, moved), :], buf.at[slot, pl.ds(0, moved), :]
        (pltpu.make_async_copy(b, a, sem.at[slot]) if to_hbm else pltpu.make_async_copy(a, b, sem.at[slot])).wait()

    @pl.when(step == 0)
    def _():
        issue_gathers(idx_ref, 0, 0)

    @pl.when(step >= 2)
    def _():
        wait_rows(o_hbm, og, (step - 2) % O_SLOTS, sem_s, True)

    @pl.when(step < last)
    def _():
        issue_gathers(idxn_ref, (step + 1) % X_SLOTS, (step + 1) % O_SLOTS)

    xslot = step % X_SLOTS
    oslot = step % O_SLOTS
    wait_rows(h_hbm, xg, xslot, sem_x, False)
    for c in range(SLAB_ROWS):
        xs_ref[:, c * LANES:(c + 1) * LANES] = xg[xslot, pl.ds(c, tc, stride=PITCH), :].astype(BF16)
    xs = xs_ref[...]
    eye = lax.broadcasted_iota(jnp.int32, (tc, tc), 0) == lax.broadcasted_iota(jnp.int32, (tc, tc), 1)
    gcol = jnp.sum(jnp.where(eye, g_ref[0], 0.0), axis=1, keepdims=True)
    for fc in range(hid_ref.shape[1] // MXU_COLS):
        fs = slice(fc * MXU_COLS, (fc + 1) * MXU_COLS)
        gate = jnp.dot(xs, wg_buf[wslot, :, fs], preferred_element_type=F32)
        up = jnp.dot(xs, wu_buf[wslot, :, fs], preferred_element_type=F32)
        hid_ref[:, fs] = (_silu(gate) * up * gcol).astype(BF16)
    hid = hid_ref[...]

    wait_rows(o_hbm, og, oslot, sem_o, False)
    per = MXU_COLS // LANES
    for dc in range(SLAB_ROWS // per):
        y = jnp.dot(hid, wd_buf[wslot, :, dc * MXU_COLS:(dc + 1) * MXU_COLS], preferred_element_type=F32)
        for k in range(per):
            c = dc * per + k
            rows = (oslot, pl.ds(c, tc, stride=PITCH), slice(None))
            og[rows] = og[rows] + g2_ref[0][:, c * LANES:(c + 1) * LANES] * y[:, k * LANES:(k + 1) * LANES]

    def scatter(src, dst):
        pltpu.make_async_copy(og.at[oslot, dst, :], o_hbm.at[src, :], sem_s.at[oslot]).start()
    for_each_slot_row(idx_ref, scatter)

    @pl.when(e + 1 < ne)
    def _():
        for cp in weight_copies(e + 1, chunk):
            cp.wait()
        cast_chunk(1 - wslot, chunk)

    @pl.when(step == last)
    def _():
        @pl.when(step >= 1)
        def _():
            wait_rows(o_hbm, og, (step - 1) % O_SLOTS, sem_s, True)
        wait_rows(o_hbm, og, oslot, sem_s, True)


def _moe_experts(idx, gate, h, wg, wu, wd, layer, gate2, x, tc):
    b, ne, cap = idx.shape
    n = x.shape[1] // PITCH
    d, f = wg.shape[2], wg.shape[3]
    nt = cap // tc
    assert b >= 2 and nt >= 2 and tc % ISSUE_UNROLL == 0 and d == SLAB_ROWS * LANES
    nsteps = ne * b * nt

    def cur(e, bi, t):
        return ((bi * ne + e) * nt + t, 0, 0)

    def nxt(e, bi, t):
        step = jnp.minimum((e * b + bi) * nt + t + 1, nsteps - 1)
        return (((step // nt) % b * ne + step // (nt * b)) * nt + step % nt, 0, 0)

    first_row = (jnp.arange(b, dtype=jnp.int32)[:, None, None] * n + idx) * PITCH
    idx3 = first_row.reshape(b * ne * nt, 1, tc)
    rows = (tc * PITCH, LANES)
    chunks = b * nt
    assert d % (16 * chunks) == 0 and f % (16 * chunks) == 0
    hbm = pl.BlockSpec(memory_space=pl.ANY)
    out = pl.pallas_call(
        functools.partial(_moe_kernel, layer=layer, nb=b, nt=nt),
        grid=(ne, b, nt),
        in_specs=[
            pl.BlockSpec((1, 1, tc), cur, memory_space=pltpu.SMEM),
            pl.BlockSpec((1, 1, tc), nxt, memory_space=pltpu.SMEM),
            pl.BlockSpec((1, 1, tc), cur),
            pl.BlockSpec((1, 1, d), lambda e, bi, t: (bi, 0, 0)),
            hbm, hbm, hbm, hbm, hbm,
        ],
        out_specs=hbm,
        out_shape=jax.ShapeDtypeStruct((b * n * PITCH, LANES), F32),
        input_output_aliases={8: 0},
        scratch_shapes=[
            pltpu.VMEM((X_SLOTS, *rows), F32), pltpu.VMEM((O_SLOTS, *rows), F32), pltpu.VMEM((tc, d), BF16),
            pltpu.VMEM((tc, f), BF16),
            pltpu.VMEM((2, d, f), BF16), pltpu.VMEM((2, d, f), BF16), pltpu.VMEM((2, f, d), BF16),
            pltpu.VMEM((d // chunks, f), F32), pltpu.VMEM((d // chunks, f), F32), pltpu.VMEM((f // chunks, d), F32),
            pltpu.SemaphoreType.DMA((X_SLOTS,)), pltpu.SemaphoreType.DMA((O_SLOTS,)),
            pltpu.SemaphoreType.DMA((O_SLOTS,)), pltpu.SemaphoreType.DMA((3,)),
        ],
        compiler_params=_params(("arbitrary", "arbitrary", "arbitrary"), disable_bounds_checks=True),
        name="moe_experts",
    )(idx3, idx3, gate.reshape(b * ne * nt, 1, tc), gate2, h.reshape(b * n * PITCH, LANES), wg, wu, wd,
      x.reshape(b * n * PITCH, LANES))
    return out.reshape(x.shape)


def _ec_moe_residual(x, h, aff_t, gate2, wg, wu, wd, layer, *, tc):
    n = x.shape[1] // PITCH
    cap = CAP_FACTOR * n // N_EXPERTS
    idx, g = _select(aff_t, cap)
    return _moe_experts(idx, g, h, wg, wu, wd, layer, gate2, x, tc)


def _slab_to_std_kernel(x_ref, o_ref):
    _slab_load_rows(x_ref, (0,), o_ref.shape[1], o_ref.at[0])


def _slab_to_std(x, d, tm):
    b = x.shape[0]
    n = x.shape[1] // PITCH
    return pl.pallas_call(
        _slab_to_std_kernel,
        grid=(b, n // tm),
        in_specs=[pl.BlockSpec((1, tm * PITCH, LANES), lambda bi, i: (bi, i, 0))],
        out_specs=pl.BlockSpec((1, tm, d), lambda bi, i: (bi, i, 0)),
        out_shape=jax.ShapeDtypeStruct((b, n, d), F32),
        compiler_params=_params(("arbitrary", "arbitrary")),
        name="slab_to_std",
    )(x)


CTX_MOE_TILE = 16


def kernel(x, c, ctx, c_ctx, ada_w, ada_b, norm1_g, norm2_g, ab_w_in, ab_w_out, a_ws, a_bs, a_vnorm_g,
           b_qnorm_g, b_knorm_g, b_rpb, c_w_in, c_conv_w, c_w_out, router_w, moe_w_gate, moe_w_up, moe_w_down):
    bsz, seq, d = x.shape
    lc = ctx.shape[1]
    depth = ada_w.shape[0]
    assert depth == 2, "layer plan: mixer A/B layer (updates the context stream) then mixer C layer"

    cond = jnp.concatenate([c, c_ctx[None], jnp.zeros((8 - bsz - 1, d), F32)], axis=0)
    mod = _ada(cond, ada_w, ada_b)

    wg_all, wu_all, wd_all = moe_w_gate, moe_w_up, moe_w_down
    xl = x
    xc = ctx
    for i in range(depth):
        upd_ctx = i < depth - 1
        sh1, sc1, g1, sh2, sc2, g2 = [mod[i, :bsz, k * d:(k + 1) * d].reshape(bsz, 1, d) for k in range(6)]
        csh1, csc1, cg1, csh2, csc2, cg2 = [
            jnp.broadcast_to(mod[i, bsz, k * d:(k + 1) * d].reshape(1, 1, d), (bsz, 1, d)) for k in range(6)]
        gam1 = norm1_g[i].reshape(1, d)
        gam2 = norm2_g[i].reshape(1, d)
        wr_split = _router_weight(router_w[i])
        j = i // 2
        if i % 2 == 0:
            w_in = ab_w_in[j].astype(BF16)
            w_out = ab_w_out[j].astype(BF16)
            ones = jnp.ones((A_WIDTH,), F32)
            gain = jnp.concatenate([
                ones, a_vnorm_g[j].reshape(-1),
                jnp.tile(b_qnorm_g[j], B_HEADS) * (B_DIM ** -0.5),
                jnp.tile(b_knorm_g[j], B_HEADS), ones]).reshape(1, -1)
            p = _ab_in(xl, gam1, sh1, sc1, w_in, gain, tm=512)
            pc = _ab_in(xc, gam1, csh1, csc1, w_in, gain, tm=lc)
            ws = a_ws[j].astype(BF16)
            bs_b = jnp.broadcast_to(a_bs[j][:, :, None], (A_GROUPS, CHUNK, A_DIM)).astype(F32)
            b_l = _neighborhood_attention(p, pc, _na_bias_table(b_rpb[j], seq // GRID_W))
            xl, h2, aff_t = _ab_out(p, ws, bs_b, b_l, w_out, xl, g1, gam2, sh2, sc2, wr_split, tm=512)
            if upd_ctx:
                b_c = _context_attention(pc)
                xc, hc2, affc_t = _ab_out(pc, ws, bs_b, b_c, w_out, xc, cg1, gam2, csh2, csc2, wr_split, tm=lc)
        else:
            assert not upd_ctx
            bg, cz = _c_in(xl, gam1, sh1, sc1, c_w_in[j].astype(BF16), tm=512)
            xl, h2, aff_t = _c_out(bg, cz, c_conv_w[j], c_w_out[j].astype(BF16), xl, g1, gam2, sh2, sc2,
                                   wr_split, tm=512)

        xl = _ec_moe_residual(xl, h2, aff_t, g2, wg_all, wu_all, wd_all, i, tc=256)
        if upd_ctx:
            xc = _ec_moe_residual(xc, hc2, affc_t, cg2, wg_all, wu_all, wd_all, i, tc=CTX_MOE_TILE)
            xc = _slab_to_std(xc, d, tm=lc)
    return _slab_to_std(xl, d, tm=512)
```

```python
import functools
import math

import jax
import jax.numpy as jnp
import numpy as np
from jax import lax
from jax.experimental import pallas as pl
from jax.experimental.pallas import tpu as pltpu

GRID_W = 64
CHUNK = 128
A_GROUPS = 8
A_DIM = 128
A_WIDTH = A_GROUPS * A_DIM
B_HEADS = 8
B_DIM = 128
B_WIDTH = B_HEADS * B_DIM
NA_ROWS = 8
NA_COLS = 16
CONV_W = 3
N_EXPERTS = 16
CAP_FACTOR = 2
EPS = 1e-6

LANES = 128
MXU_COLS = 256
VMEM_LIMIT = 56 * 1024 * 1024

F32 = jnp.float32
BF16 = jnp.bfloat16
HIGHEST = lax.Precision.HIGHEST
MASK_VALUE = -1e30


SLAB_ROWS = 16
PITCH = 20


def _params(sem, **kw):
    return pltpu.CompilerParams(dimension_semantics=sem, vmem_limit_bytes=VMEM_LIMIT, **kw)


def _slab_cols(ref, lead, c, tm):
    return (*lead, pl.ds(c, tm, stride=PITCH), slice(None))


def _slab_zero_pad(ref, lead, tm):
    for c in range(SLAB_ROWS, PITCH):
        ref[_slab_cols(ref, lead, c, tm)] = jnp.zeros((tm, LANES), ref.dtype)


def _slab_load_rows(ref, lead, tm, dst_ref):
    for c in range(SLAB_ROWS):
        dst_ref[:, c * LANES:(c + 1) * LANES] = ref[_slab_cols(ref, lead, c, tm)]


def _silu(x):
    return x * (1.0 / (1.0 + jnp.exp(-x)))


def _gelu_tanh(x):
    return 0.5 * x * (1.0 + jnp.tanh(math.sqrt(2.0 / math.pi) * (x + 0.044715 * (x * x * x))))


def _norm_mod(x, gamma, shift, scale):
    ms = jnp.mean(x * x, axis=-1, keepdims=True)
    return (x * lax.rsqrt(ms + EPS) * gamma) * (1.0 + scale) + shift


def _ada_kernel(c_ref, w_ref, b_ref, o_ref):
    s = _silu(c_ref[...])
    o_ref[0] = jnp.dot(s, w_ref[0], precision=HIGHEST, preferred_element_type=F32) + b_ref[0]


def _ada(cond, ada_w, ada_b):
    depth, d, n6 = ada_w.shape
    tn = 1024
    return pl.pallas_call(
        _ada_kernel,
        grid=(depth, n6 // tn),
        in_specs=[
            pl.BlockSpec((8, d), lambda l, j: (0, 0)),
            pl.BlockSpec((1, d, tn), lambda l, j: (l, 0, j)),
            pl.BlockSpec((1, 1, tn), lambda l, j: (l, 0, j)),
        ],
        out_specs=pl.BlockSpec((1, 8, tn), lambda l, j: (l, 0, j)),
        out_shape=jax.ShapeDtypeStruct((depth, 8, n6), F32),
        compiler_params=_params(("arbitrary", "arbitrary")),
        name="ada_mod",
    )(cond, ada_w, ada_b.reshape(depth, 1, n6))


AB_SEGMENT_EPILOGUES = ("gelu", "gelu_norm", "norm", "norm", "none")


def _ab_in_kernel(x_ref, gam_ref, sh_ref, sc_ref, w_ref, gain_ref, o_ref, h_ref, *, tn):
    h_ref[...] = _norm_mod(x_ref[0], gam_ref[...], sh_ref[0], sc_ref[0]).astype(BF16)
    per_seg = A_WIDTH // tn
    for jt in range(w_ref.shape[1] // tn):
        kind = AB_SEGMENT_EPILOGUES[jt // per_seg]
        acc = jnp.dot(h_ref[...], w_ref[:, jt * tn:(jt + 1) * tn], preferred_element_type=F32)
        if kind.startswith("gelu"):
            acc = _gelu_tanh(acc)
        if kind.endswith("norm"):
            for g in range(tn // LANES):
                sl = slice(jt * tn + g * LANES, jt * tn + (g + 1) * LANES)
                ag = acc[:, g * LANES:(g + 1) * LANES]
                ms = jnp.mean(ag * ag, axis=-1, keepdims=True)
                o_ref[0, :, sl] = (ag * lax.rsqrt(ms + EPS) * gain_ref[:, sl]).astype(o_ref.dtype)
        else:
            o_ref[0, :, jt * tn:(jt + 1) * tn] = acc.astype(o_ref.dtype)


def _resident(shape):
    nd = len(shape)
    return pl.BlockSpec(shape, lambda *_: (0,) * nd, pipeline_mode=pl.Buffered(1))


def _ab_in(x, gamma, shift, scale, w, gain, tm, tn=512):
    b, n, d = x.shape
    f = w.shape[1]
    return pl.pallas_call(
        functools.partial(_ab_in_kernel, tn=tn),
        grid=(b, n // tm),
        in_specs=[
            pl.BlockSpec((1, tm, d), lambda bi, i: (bi, i, 0)),
            _resident((1, d)),
            pl.BlockSpec((1, 1, d), lambda bi, i: (bi, 0, 0)),
            pl.BlockSpec((1, 1, d), lambda bi, i: (bi, 0, 0)),
            _resident((d, f)),
            _resident((1, f)),
        ],
        out_specs=pl.BlockSpec((1, tm, f), lambda bi, i: (bi, i, 0)),
        out_shape=jax.ShapeDtypeStruct((b, n, f), BF16),
        scratch_shapes=[pltpu.VMEM((tm, d), BF16)],
        compiler_params=_params(("arbitrary", "arbitrary")),
        name="ab_in_proj",
    )(x, gamma, shift, scale, w, gain)


def _gmlp_tile(u_ref, v_ref, ws_ref, bs_ref, a_ref):
    for ch in range(a_ref.shape[0] // CHUNK):
        rows = slice(ch * CHUNK, (ch + 1) * CHUNK)
        for g in range(A_GROUPS):
            cols = slice(g * A_DIM, (g + 1) * A_DIM)
            s = jnp.dot(ws_ref[g], v_ref[0, rows, cols], preferred_element_type=F32) + bs_ref[g]
            a_ref[rows, cols] = (u_ref[0, rows, cols].astype(F32) * s).astype(a_ref.dtype)


Q_ROWS = 4
KV_BLOCKS = 3
MASKED_PLANE = 2 * NA_ROWS - 1


def _na_bias_table(rpb, rows):
    h = rpb.shape[0]
    cols = np.arange(GRID_W)
    cstart = np.clip(cols - NA_COLS // 2, 0, GRID_W - NA_COLS)
    valid = (cols[None, :] >= cstart[:, None]) & (cols[None, :] < cstart[:, None] + NA_COLS)
    col_off = np.clip(cols[None, :] - cols[:, None] + (NA_COLS - 1), 0, 2 * NA_COLS - 2)
    planes = jnp.where(valid[None, None], rpb[:, :, col_off], MASK_VALUE)
    planes = jnp.concatenate([planes, jnp.full((h, 1, GRID_W, GRID_W), MASK_VALUE, planes.dtype)], axis=1)
    nblk = rows // Q_ROWS
    assert nblk >= KV_BLOCKS + 1 and NA_ROWS <= (KV_BLOCKS - 1) * Q_ROWS
    sel = np.full((3, Q_ROWS, KV_BLOCKS * Q_ROWS), MASKED_PLANE, np.int32)
    for variant, g in enumerate((0, 1, nblk - 1)):
        first = min(max(g - 1, 0), nblk - KV_BLOCKS)
        for i in range(Q_ROWS):
            r = g * Q_ROWS + i
            rs = min(max(r - NA_ROWS // 2, 0), rows - NA_ROWS)
            for kr in range(KV_BLOCKS * Q_ROWS):
                key_row = first * Q_ROWS + kr
                if rs <= key_row < rs + NA_ROWS:
                    sel[variant, i, kr] = key_row - r + (NA_ROWS - 1)
    t = planes[:, sel]
    t = jnp.transpose(t, (1, 0, 2, 4, 3, 5))
    return t.reshape(3, h, Q_ROWS * GRID_W, KV_BLOCKS * Q_ROWS * GRID_W).astype(F32)


def _na_kernel(q_ref, k0_ref, k1_ref, k2_ref, v0_ref, v1_ref, v2_ref, kx_ref, vx_ref, bias_ref, o_ref):
    dn = (((1,), (1,)), ((), ()))
    tk = k0_ref.shape[1]
    k_refs = (k0_ref, k1_ref, k2_ref)
    v_refs = (v0_ref, v1_ref, v2_ref)
    for h in range(B_HEADS):
        cols = slice(h * B_DIM, (h + 1) * B_DIM)
        q = q_ref[0, :, cols]
        s = [lax.dot_general(q, k_refs[n][0, :, cols], dn, preferred_element_type=F32)
             + bias_ref[0, h, :, n * tk:(n + 1) * tk] for n in range(KV_BLOCKS)]
        s.append(lax.dot_general(q, kx_ref[0, :, cols], dn, preferred_element_type=F32))
        m = functools.reduce(jnp.maximum, [jnp.max(x, axis=-1, keepdims=True) for x in s])
        p = [jnp.exp(x - m) for x in s]
        l = functools.reduce(jnp.add, [jnp.sum(x, axis=-1, keepdims=True) for x in p])
        o = jnp.dot(p[KV_BLOCKS].astype(BF16), vx_ref[0, :, cols], preferred_element_type=F32)
        for n in range(KV_BLOCKS):
            o += jnp.dot(p[n].astype(BF16), v_refs[n][0, :, cols], preferred_element_type=F32)
        o_ref[0, :, cols] = (o / l).astype(o_ref.dtype)


def _neighborhood_attention(p, pc, bias):
    b, s, _ = p.shape
    lc = pc.shape[1]
    tq = Q_ROWS * GRID_W
    nblk = s // tq
    qcol, kcol, vcol = 2, 3, 4
    first = lambda i: jnp.clip(i - 1, 0, nblk - KV_BLOCKS)
    kv = lambda col, n: pl.BlockSpec((1, tq, B_WIDTH), lambda bi, i: (bi, first(i) + n, col))
    variant = lambda i: jnp.where(i == 0, 0, jnp.where(i == nblk - 1, 2, 1))
    return pl.pallas_call(
        _na_kernel,
        grid=(b, nblk),
        in_specs=[
            pl.BlockSpec((1, tq, B_WIDTH), lambda bi, i: (bi, i, qcol)),
            *[kv(kcol, n) for n in range(KV_BLOCKS)],
            *[kv(vcol, n) for n in range(KV_BLOCKS)],
            pl.BlockSpec((1, lc, B_WIDTH), lambda bi, i: (bi, 0, kcol)),
            pl.BlockSpec((1, lc, B_WIDTH), lambda bi, i: (bi, 0, vcol)),
            pl.BlockSpec((1, *bias.shape[1:]), lambda bi, i: (variant(i), 0, 0, 0)),
        ],
        out_specs=pl.BlockSpec((1, tq, B_WIDTH), lambda bi, i: (bi, i, 0)),
        out_shape=jax.ShapeDtypeStruct((b, s, B_WIDTH), BF16),
        compiler_params=_params(("arbitrary", "arbitrary")),
        name="neighborhood_attention",
    )(p, *([p] * (2 * KV_BLOCKS)), pc, pc, bias)


def _ctx_attn_kernel(q_ref, k_ref, v_ref, o_ref):
    for h in range(B_HEADS):
        cols = slice(h * B_DIM, (h + 1) * B_DIM)
        s = lax.dot_general(q_ref[0, :, cols], k_ref[0, :, cols], (((1,), (1,)), ((), ())),
                            preferred_element_type=F32)
        p = jnp.exp(s - jnp.max(s, axis=-1, keepdims=True))
        o = jnp.dot(p.astype(BF16), v_ref[0, :, cols], preferred_element_type=F32)
        o_ref[0, :, cols] = (o / jnp.sum(p, axis=-1, keepdims=True)).astype(o_ref.dtype)


def _context_attention(pc):
    b, lc, _ = pc.shape
    spec = lambda col: pl.BlockSpec((1, lc, B_WIDTH), lambda bi: (bi, 0, col))
    return pl.pallas_call(
        _ctx_attn_kernel,
        grid=(b,),
        in_specs=[spec(2), spec(3), spec(4)],
        out_specs=pl.BlockSpec((1, lc, B_WIDTH), lambda bi: (bi, 0, 0)),
        out_shape=jax.ShapeDtypeStruct((b, lc, B_WIDTH), BF16),
        compiler_params=_params(("arbitrary",)),
        name="context_attention",
    )(pc, pc, pc)


def _store_residual_and_route(xf_ref, gam_ref, sh_ref, sc_ref, wr_ref, o_ref, h_ref, aff_ref):
    tm = xf_ref.shape[0]
    xn = xf_ref[...]
    h = _norm_mod(xn, gam_ref[...], sh_ref[0], sc_ref[0])
    _slab_zero_pad(o_ref, (0,), tm)
    _slab_zero_pad(h_ref, (0,), tm)
    for c in range(SLAB_ROWS):
        sl = slice(c * LANES, (c + 1) * LANES)
        o_ref[_slab_cols(o_ref, (0,), c, tm)] = xn[:, sl]
        h_ref[_slab_cols(h_ref, (0,), c, tm)] = h[:, sl]
    ne = aff_ref.shape[1]
    h_hi = h.astype(BF16)
    h_lo = (h - h_hi.astype(F32)).astype(BF16)
    p_hi = jnp.dot(h_hi, wr_ref[...], preferred_element_type=F32).T
    p_lo = jnp.dot(h_lo, wr_ref[...], preferred_element_type=F32).T
    logits = p_hi[0:ne] + (p_hi[ne:2 * ne] + p_lo[0:ne])
    m = jnp.max(logits, axis=0, keepdims=True)
    e = jnp.exp(logits - m)
    aff_ref[0] = e / jnp.sum(e, axis=0, keepdims=True)


def _router_weight(router_w):
    d, ne = router_w.shape
    w_hi = router_w.astype(BF16)
    w_lo = (router_w - w_hi.astype(F32)).astype(BF16)
    return jnp.concatenate([w_hi, w_lo, jnp.zeros((d, LANES - 2 * ne), BF16)], axis=1)


def _route_specs(b, n, d, ne, tm):
    slab = pl.BlockSpec((1, tm * PITCH, LANES), lambda bi, i: (bi, i, 0))
    mod = pl.BlockSpec((1, 1, d), lambda bi, i: (bi, 0, 0))
    in_specs = [_resident((1, d)), mod, mod, _resident((d, LANES))]
    out_specs = [slab, slab, pl.BlockSpec((1, ne, tm), lambda bi, i: (bi, 0, i))]
    slab_shape = jax.ShapeDtypeStruct((b, n * PITCH, LANES), F32)
    return in_specs, out_specs, [slab_shape, slab_shape, jax.ShapeDtypeStruct((b, ne, n), F32)]


def _ab_out_kernel(u_ref, v_ref, ws_ref, bs_ref, b_ref, w_ref, x_ref, g_ref, gam_ref, sh_ref, sc_ref, wr_ref,
                   o_ref, h_ref, aff_ref, xf_ref, a_ref, *, tn):
    _gmlp_tile(u_ref, v_ref, ws_ref, bs_ref, a_ref)
    ka = a_ref.shape[1]
    for jt in range(w_ref.shape[1] // tn):
        cols = slice(jt * tn, (jt + 1) * tn)
        acc = jnp.dot(a_ref[...], w_ref[0:ka, cols], preferred_element_type=F32)
        acc += jnp.dot(b_ref[0], w_ref[ka:2 * ka, cols], preferred_element_type=F32)
        xf_ref[:, cols] = x_ref[0, :, cols] + g_ref[0, :, cols] * acc
    _store_residual_and_route(xf_ref, gam_ref, sh_ref, sc_ref, wr_ref, o_ref, h_ref, aff_ref)


def _ab_out(p, ws, bs_b, bm, w, x, gate, gamma2, shift2, scale2, wr_split, tm, tn=512):
    b, n, d = x.shape
    rin, rout, rshape = _route_specs(b, n, d, N_EXPERTS, tm)
    return pl.pallas_call(
        functools.partial(_ab_out_kernel, tn=tn),
        grid=(b, n // tm),
        in_specs=[
            pl.BlockSpec((1, tm, A_WIDTH), lambda bi, i: (bi, i, 0)),
            pl.BlockSpec((1, tm, A_WIDTH), lambda bi, i: (bi, i, 1)),
            _resident(ws.shape),
            _resident(bs_b.shape),
            pl.BlockSpec((1, tm, B_WIDTH), lambda bi, i: (bi, i, 0)),
            _resident(w.shape),
            pl.BlockSpec((1, tm, d), lambda bi, i: (bi, i, 0)),
            pl.BlockSpec((1, 1, d), lambda bi, i: (bi, 0, 0)),
            *rin,
        ],
        out_specs=rout,
        out_shape=rshape,
        scratch_shapes=[pltpu.VMEM((tm, d), F32), pltpu.VMEM((tm, A_WIDTH), BF16)],
        compiler_params=_params(("arbitrary", "arbitrary")),
        name="ab_out_proj",
    )(p, p, ws, bs_b, bm, w, x, gate, gamma2, shift2, scale2, wr_split)


def _c_in_kernel(x_ref, gam_ref, sh_ref, sc_ref, w_ref, bg_ref, cz_ref, h_ref, xf_ref, *, tn):
    _slab_load_rows(x_ref, (0,), h_ref.shape[0], xf_ref)
    h_ref[...] = _norm_mod(xf_ref[...], gam_ref[...], sh_ref[0], sc_ref[0]).astype(BF16)
    cw = w_ref.shape[1] // 3
    for jt in range(cw // tn):
        cols = slice(jt * tn, (jt + 1) * tn)
        proj = lambda seg: jnp.dot(h_ref[...], w_ref[:, seg * cw + jt * tn:seg * cw + (jt + 1) * tn],
                                   preferred_element_type=F32)
        bg_ref[0, :, cols] = proj(0).astype(bg_ref.dtype)
        cz_ref[0, :, cols] = (proj(1) * proj(2)).astype(cz_ref.dtype)


def _c_in(x, gamma, shift, scale, w, tm, tn=512):
    b = x.shape[0]
    n = x.shape[1] // PITCH
    d = w.shape[0]
    cw = w.shape[1] // 3
    out = jax.ShapeDtypeStruct((b, n, cw), BF16)
    ospec = pl.BlockSpec((1, tm, cw), lambda bi, i: (bi, i, 0))
    return pl.pallas_call(
        functools.partial(_c_in_kernel, tn=tn),
        grid=(b, n // tm),
        in_specs=[
            pl.BlockSpec((1, tm * PITCH, LANES), lambda bi, i: (bi, i, 0)),
            _resident((1, d)),
            pl.BlockSpec((1, 1, d), lambda bi, i: (bi, 0, 0)),
            pl.BlockSpec((1, 1, d), lambda bi, i: (bi, 0, 0)),
            _resident(w.shape),
        ],
        out_specs=[ospec, ospec],
        out_shape=[out, out],
        scratch_shapes=[pltpu.VMEM((tm, d), BF16), pltpu.VMEM((tm, d), F32)],
        compiler_params=_params(("arbitrary", "arbitrary")),
        name="c_in_proj",
    )(x, gamma, shift, scale, w)


HALO = 8


def _c_out_kernel(bg_ref, cz_ref, czp_ref, czn_ref, cw_ref, w_ref, x_ref, g_ref, gam_ref, sh_ref, sc_ref, wr_ref,
                  o_ref, h_ref, aff_ref, xf_ref, acc_ref, *, kchunk, tn):
    i = pl.program_id(1)
    ni = pl.num_programs(1)
    tm = cz_ref.shape[1]
    cw = cz_ref.shape[2]
    _slab_load_rows(x_ref, (0,), tm, xf_ref)
    row = lax.broadcasted_iota(jnp.int32, (tm, kchunk), 0)
    for kc in range(cw // kchunk):
        ks = slice(kc * kchunk, (kc + 1) * kchunk)
        cz = cz_ref[0, :, ks].astype(F32)
        prev_row = jnp.where(i > 0, czp_ref[0, HALO - 1:HALO, ks].astype(F32), 0.0)
        next_row = jnp.where(i < ni - 1, czn_ref[0, 0:1, ks].astype(F32), 0.0)
        up = jnp.where(row == 0, prev_row, pltpu.roll(cz, 1, 0))
        dn = jnp.where(row == tm - 1, next_row, pltpu.roll(cz, tm - 1, 0))
        y = cw_ref[0:1, ks] * up + cw_ref[1:2, ks] * cz + cw_ref[2:3, ks] * dn
        lhs = (bg_ref[0, :, ks].astype(F32) * y).astype(BF16)
        for jt in range(w_ref.shape[1] // tn):
            cols = slice(jt * tn, (jt + 1) * tn)
            part = jnp.dot(lhs, w_ref[ks, cols], preferred_element_type=F32)
            if kc == 0:
                acc_ref[:, cols] = part
            else:
                acc_ref[:, cols] += part
    xf_ref[...] = xf_ref[...] + g_ref[0] * acc_ref[...]
    _store_residual_and_route(xf_ref, gam_ref, sh_ref, sc_ref, wr_ref, o_ref, h_ref, aff_ref)


def _c_out(bg, cz, conv_w, w, x, gate, gamma2, shift2, scale2, wr_split, tm, tn=512):
    b, n, cw = bg.shape
    d = w.shape[1]
    hb = tm // HALO
    nh = n // HALO
    rin, rout, rshape = _route_specs(b, n, d, N_EXPERTS, tm)
    return pl.pallas_call(
        functools.partial(_c_out_kernel, kchunk=512, tn=tn),
        grid=(b, n // tm),
        in_specs=[
            pl.BlockSpec((1, tm, cw), lambda bi, i: (bi, i, 0)),
            pl.BlockSpec((1, tm, cw), lambda bi, i: (bi, i, 0)),
            pl.BlockSpec((1, HALO, cw), lambda bi, i: (bi, jnp.maximum(i * hb - 1, 0), 0)),
            pl.BlockSpec((1, HALO, cw), lambda bi, i: (bi, jnp.minimum((i + 1) * hb, nh - 1), 0)),
            _resident((CONV_W, cw)),
            _resident(w.shape),
            pl.BlockSpec((1, tm * PITCH, LANES), lambda bi, i: (bi, i, 0)),
            pl.BlockSpec((1, 1, d), lambda bi, i: (bi, 0, 0)),
            *rin,
        ],
        out_specs=rout,
        out_shape=rshape,
        scratch_shapes=[pltpu.VMEM((tm, d), F32), pltpu.VMEM((tm, d), F32)],
        compiler_params=_params(("arbitrary", "arbitrary")),
        name="c_out_proj",
    )(bg, cz, cz, cz, conv_w, w, x, gate, gamma2, shift2, scale2, wr_split)


SEARCH_BITS = 3
SELECT_GROUP = 8


def _select_kernel(aff_ref, idx_ref, gate_ref, *, cap, cchunk):
    group = aff_ref.shape[1]
    vs = [aff_ref[0, g] for g in range(group)]

    def count(mask):
        return jnp.sum(jnp.sum(mask.astype(F32), axis=1, keepdims=True), axis=0, keepdims=True)

    def search(it, ts):
        shift = 30 - SEARCH_BITS * (it + 1)
        out = []
        for v, t in zip(vs, ts):
            digit = jnp.zeros((1, 1), jnp.int32)
            for j in range(1, 2 ** SEARCH_BITS):
                cand = pltpu.bitcast(t | (jnp.int32(j) << shift), F32)
                digit += (count(v >= cand) >= cap).astype(jnp.int32)
            out.append(t | (digit << shift))
        return tuple(out)

    thrs = lax.fori_loop(0, 30 // SEARCH_BITS, search, tuple(jnp.zeros((1, 1), jnp.int32) for _ in vs))
    for g, (v, thr_bits) in enumerate(zip(vs, thrs)):
        _compact_selected(v, pltpu.bitcast(thr_bits, F32), idx_ref.at[0, g], gate_ref.at[0, g], cap, cchunk, count)


def _compact_selected(v, thr, idx_ref, gate_ref, cap, cchunk, count):
    nr = v.shape[0]
    gt = v > thr
    eq = v == thr
    need = cap - count(gt)

    lane_l = lax.broadcasted_iota(jnp.int32, (LANES, LANES), 0)
    lane_c = lax.broadcasted_iota(jnp.int32, (LANES, LANES), 1)
    tri_lane = (lane_l <= lane_c).astype(BF16)
    row_r = lax.broadcasted_iota(jnp.int32, (nr, nr), 0)
    row_c = lax.broadcasted_iota(jnp.int32, (nr, nr), 1)
    tri_row = (row_c <= row_r).astype(BF16)

    def prefix(mask):
        mf = mask.astype(BF16)
        in_row = jnp.dot(mf, tri_lane, preferred_element_type=F32)
        colcum = jnp.dot(tri_row, mf, preferred_element_type=F32)
        row_incl = jnp.sum(colcum, axis=1, keepdims=True)
        row_tot = jnp.sum(mask.astype(F32), axis=1, keepdims=True)
        return in_row, row_incl - row_tot, row_incl

    eq_in, eq_off, _ = prefix(eq)
    eq_rank = eq_in + eq_off - eq.astype(F32)
    sel = gt | (eq & (eq_rank < need))
    _, sel_off, sel_incl = prefix(sel)

    self_bf = sel.astype(BF16)
    tri_lane_t = (lane_c <= lane_l).astype(BF16)
    dn_t = (((1,), (1,)), ((), ()))
    pt = lax.dot_general(tri_lane_t, self_bf, dn_t, preferred_element_type=F32)
    eye = (lane_l == lane_c).astype(F32)
    vt = lax.dot_general(eye, v, dn_t, precision=HIGHEST, preferred_element_type=F32)

    for c0 in range(0, cap, cchunk):
        cc = min(cchunk, cap - c0)
        slot = (lax.broadcasted_iota(jnp.int32, (1, cc), 1) + c0).astype(F32)
        r_of = jnp.sum((sel_incl <= slot).astype(F32), axis=0, keepdims=True)
        onehot = (lax.broadcasted_iota(jnp.int32, (nr, cc), 0).astype(F32) == r_of)
        onehot_f = onehot.astype(F32)
        local = slot - jnp.sum(onehot_f * sel_off, axis=0, keepdims=True)
        prow = jnp.dot(pt.astype(BF16), onehot.astype(BF16), preferred_element_type=F32)
        l_of = jnp.sum((prow <= local).astype(F32), axis=0, keepdims=True)
        vrow = jnp.dot(vt, onehot_f, precision=HIGHEST, preferred_element_type=F32)
        lane_i = lax.broadcasted_iota(jnp.int32, (LANES, cc), 0).astype(F32)
        gsel = jnp.sum(jnp.where(lane_i == l_of, vrow, 0.0), axis=0, keepdims=True)
        idx_ref[:, c0:c0 + cc] = (r_of * LANES + l_of).astype(jnp.int32)
        gate_ref[:, c0:c0 + cc] = gsel


MIN_SELECT_ROWS = 8


def _select(aff_t, cap):
    b, ne, n = aff_t.shape
    if n < MIN_SELECT_ROWS * LANES:
        assert cap <= n
        aff_t = jnp.pad(aff_t, ((0, 0), (0, 0), (0, MIN_SELECT_ROWS * LANES - n)), constant_values=-1.0)
        n = MIN_SELECT_ROWS * LANES
    nr = n // LANES
    group = SELECT_GROUP
    assert ne % group == 0
    out = lambda dt: jax.ShapeDtypeStruct((b, ne, 1, cap), dt)
    ospec = pl.BlockSpec((1, group, 1, cap), lambda bi, e: (bi, e, 0, 0))
    idx, gate = pl.pallas_call(
        functools.partial(_select_kernel, cap=cap, cchunk=512),
        grid=(b, ne // group),
        in_specs=[pl.BlockSpec((1, group, nr, LANES), lambda bi, e: (bi, e, 0, 0))],
        out_specs=[ospec, ospec],
        out_shape=[out(jnp.int32), out(F32)],
        compiler_params=_params(("arbitrary", "arbitrary")),
        name="moe_select",
    )(aff_t.reshape(b, ne, nr, LANES))
    return idx.reshape(b, ne, cap), gate.reshape(b, ne, cap)


X_SLOTS = 2
O_SLOTS = 3
GATHER_UNROLL = 16
SCATTER_UNROLL = 8


def _moe_kernel(idx_ref, idxn_ref, g_ref, g2_ref, h_hbm, wg_hbm, wu_hbm, wd_hbm, x_hbm, o_hbm,
                xg, og, xs_ref, hid_ref, wg_buf, wu_buf, wd_buf, wg_stage, wu_stage, wd_stage, sem_x, sem_o, sem_s, sem_w,
                *, layer, nb, nt):
    del x_hbm
    e, bi, t = pl.program_id(0), pl.program_id(1), pl.program_id(2)
    ne = pl.num_programs(0)
    step = (e * nb + bi) * nt + t
    last = ne * nb * nt - 1
    tc = xs_ref.shape[0]
    moved = tc * SLAB_ROWS

    chunks = nb * nt
    chunk = bi * nt + t
    wslot = e % 2
    stages = (wg_stage, wu_stage, wd_stage)
    bufs = (wg_buf, wu_buf, wd_buf)

    def weight_rows(ref, k):
        rows = ref.shape[0]
        return pl.ds(pl.multiple_of(k * rows, rows), rows)

    def weight_copies(expert, k):
        return [pltpu.make_async_copy(hbm.at[layer, expert, weight_rows(stage, k), :], stage, sem_w.at[i])
                for i, (hbm, stage) in enumerate(zip((wg_hbm, wu_hbm, wd_hbm), stages))]

    def cast_chunk(slot, k):
        for stage, buf in zip(stages, bufs):
            buf[slot, weight_rows(stage, k), :] = stage[...].astype(BF16)

    @pl.when(step == 0)
    def _():
        def load(k, carry):
            copies = weight_copies(0, k)
            for cp in copies:
                cp.start()
            for cp in copies:
                cp.wait()
            cast_chunk(0, k)
            return carry
        lax.fori_loop(0, chunks, load, 0)

    @pl.when(e + 1 < ne)
    def _():
        for cp in weight_copies(e + 1, chunk):
            cp.start()

    def for_each_slot_row(row_ref, fn, unroll):
        def body(it, carry):
            for u in range(unroll):
                s = it * unroll + u
                fn(pl.ds(pl.multiple_of(row_ref[0, 0, s], 4), SLAB_ROWS),
                   pl.ds(pl.multiple_of(s * PITCH, 4), SLAB_ROWS))
            return carry
        lax.fori_loop(0, tc // unroll, body, 0)

    def issue_gathers(row_ref, xslot, oslot):
        def one(src, dst):
            pltpu.make_async_copy(h_hbm.at[src, :], xg.at[xslot, dst, :], sem_x.at[xslot]).start()
            pltpu.make_async_copy(o_hbm.at[src, :], og.at[oslot, dst, :], sem_o.at[oslot]).start()
        for_each_slot_row(row_ref, one, GATHER_UNROLL)

    def wait_rows(hbm, buf, slot, sem, to_hbm):
        a, b = hbm.at[pl.ds(0, moved), :], buf.at[slot, pl.ds(0, moved), :]
        (pltpu.make_async_copy(b, a, sem.at[slot]) if to_hbm else pltpu.make_async_copy(a, b, sem.at[slot])).wait()

    @pl.when(step == 0)
    def _():
        issue_gathers(idx_ref, 0, 0)

    @pl.when(step >= 2)
    def _():
        wait_rows(o_hbm, og, (step - 2) % O_SLOTS, sem_s, True)

    @pl.when(step < last)
    def _():
        issue_gathers(idxn_ref, (step + 1) % X_SLOTS, (step + 1) % O_SLOTS)

    xslot = step % X_SLOTS
    oslot = step % O_SLOTS
    wait_rows(h_hbm, xg, xslot, sem_x, False)
    for c in range(SLAB_ROWS):
        xs_ref[:, c * LANES:(c + 1) * LANES] = xg[xslot, pl.ds(c, tc, stride=PITCH), :].astype(BF16)
    xs = xs_ref[...]
    eye = lax.broadcasted_iota(jnp.int32, (tc, tc), 0) == lax.broadcasted_iota(jnp.int32, (tc, tc), 1)
    gcol = jnp.sum(jnp.where(eye, g_ref[0], 0.0), axis=1, keepdims=True)
    for fc in range(hid_ref.shape[1] // MXU_COLS):
        fs = slice(fc * MXU_COLS, (fc + 1) * MXU_COLS)
        gate = jnp.dot(xs, wg_buf[wslot, :, fs], preferred_element_type=F32)
        up = jnp.dot(xs, wu_buf[wslot, :, fs], preferred_element_type=F32)
        hid_ref[:, fs] = (_silu(gate) * up * gcol).astype(BF16)
    hid = hid_ref[...]

    wait_rows(o_hbm, og, oslot, sem_o, False)
    per = MXU_COLS // LANES
    for dc in range(SLAB_ROWS // per):
        y = jnp.dot(hid, wd_buf[wslot, :, dc * MXU_COLS:(dc + 1) * MXU_COLS], preferred_element_type=F32)
        for k in range(per):
            c = dc * per + k
            rows = (oslot, pl.ds(c, tc, stride=PITCH), slice(None))
            og[rows] = og[rows] + g2_ref[0][:, c * LANES:(c + 1) * LANES] * y[:, k * LANES:(k + 1) * LANES]

    def scatter(src, dst):
        pltpu.make_async_copy(og.at[oslot, dst, :], o_hbm.at[src, :], sem_s.at[oslot]).start()
    for_each_slot_row(idx_ref, scatter, SCATTER_UNROLL)

    @pl.when(e + 1 < ne)
    def _():
        for cp in weight_copies(e + 1, chunk):
            cp.wait()
        cast_chunk(1 - wslot, chunk)

    @pl.when(step == last)
    def _():
        @pl.when(step >= 1)
        def _():
            wait_rows(o_hbm, og, (step - 1) % O_SLOTS, sem_s, True)
        wait_rows(o_hbm, og, oslot, sem_s, True)


def _moe_experts(idx, gate, h, wg, wu, wd, layer, gate2, x, tc):
    b, ne, cap = idx.shape
    n = x.shape[1] // PITCH
    d, f = wg.shape[2], wg.shape[3]
    nt = cap // tc
    assert b >= 2 and nt >= 2 and tc % GATHER_UNROLL == 0 and tc % SCATTER_UNROLL == 0 and d == SLAB_ROWS * LANES
    nsteps = ne * b * nt

    def cur(e, bi, t):
        return ((bi * ne + e) * nt + t, 0, 0)

    def nxt(e, bi, t):
        step = jnp.minimum((e * b + bi) * nt + t + 1, nsteps - 1)
        return (((step // nt) % b * ne + step // (nt * b)) * nt + step % nt, 0, 0)

    first_row = (jnp.arange(b, dtype=jnp.int32)[:, None, None] * n + idx) * PITCH
    idx3 = first_row.reshape(b * ne * nt, 1, tc)
    rows = (tc * PITCH, LANES)
    chunks = b * nt
    assert d % (16 * chunks) == 0 and f % (16 * chunks) == 0
    hbm = pl.BlockSpec(memory_space=pl.ANY)
    out = pl.pallas_call(
        functools.partial(_moe_kernel, layer=layer, nb=b, nt=nt),
        grid=(ne, b, nt),
        in_specs=[
            pl.BlockSpec((1, 1, tc), cur, memory_space=pltpu.SMEM),
            pl.BlockSpec((1, 1, tc), nxt, memory_space=pltpu.SMEM),
            pl.BlockSpec((1, 1, tc), cur),
            pl.BlockSpec((1, 1, d), lambda e, bi, t: (bi, 0, 0)),
            hbm, hbm, hbm, hbm, hbm,
        ],
        out_specs=hbm,
        out_shape=jax.ShapeDtypeStruct((b * n * PITCH, LANES), F32),
        input_output_aliases={8: 0},
        scratch_shapes=[
            pltpu.VMEM((X_SLOTS, *rows), F32), pltpu.VMEM((O_SLOTS, *rows), F32), pltpu.VMEM((tc, d), BF16),
            pltpu.VMEM((tc, f), BF16),
            pltpu.VMEM((2, d, f), BF16), pltpu.VMEM((2, d, f), BF16), pltpu.VMEM((2, f, d), BF16),
            pltpu.VMEM((d // chunks, f), F32), pltpu.VMEM((d // chunks, f), F32), pltpu.VMEM((f // chunks, d), F32),
            pltpu.SemaphoreType.DMA((X_SLOTS,)), pltpu.SemaphoreType.DMA((O_SLOTS,)),
            pltpu.SemaphoreType.DMA((O_SLOTS,)), pltpu.SemaphoreType.DMA((3,)),
        ],
        compiler_params=_params(("arbitrary", "arbitrary", "arbitrary"), disable_bounds_checks=True),
        name="moe_experts",
    )(idx3, idx3, gate.reshape(b * ne * nt, 1, tc), gate2, h.reshape(b * n * PITCH, LANES), wg, wu, wd,
      x.reshape(b * n * PITCH, LANES))
    return out.reshape(x.shape)


def _ec_moe_residual(x, h, aff_t, gate2, wg, wu, wd, layer, *, tc):
    n = x.shape[1] // PITCH
    cap = CAP_FACTOR * n // N_EXPERTS
    idx, g = _select(aff_t, cap)
    return _moe_experts(idx, g, h, wg, wu, wd, layer, gate2, x, tc)


def _slab_to_std_kernel(x_ref, o_ref):
    _slab_load_rows(x_ref, (0,), o_ref.shape[1], o_ref.at[0])


def _slab_to_std(x, d, tm):
    b = x.shape[0]
    n = x.shape[1] // PITCH
    return pl.pallas_call(
        _slab_to_std_kernel,
        grid=(b, n // tm),
        in_specs=[pl.BlockSpec((1, tm * PITCH, LANES), lambda bi, i: (bi, i, 0))],
        out_specs=pl.BlockSpec((1, tm, d), lambda bi, i: (bi, i, 0)),
        out_shape=jax.ShapeDtypeStruct((b, n, d), F32),
        compiler_params=_params(("arbitrary", "arbitrary")),
        name="slab_to_std",
    )(x)


CTX_MOE_TILE = 16


def kernel(x, c, ctx, c_ctx, ada_w, ada_b, norm1_g, norm2_g, ab_w_in, ab_w_out, a_ws, a_bs, a_vnorm_g,
           b_qnorm_g, b_knorm_g, b_rpb, c_w_in, c_conv_w, c_w_out, router_w, moe_w_gate, moe_w_up, moe_w_down):
    bsz, seq, d = x.shape
    lc = ctx.shape[1]
    depth = ada_w.shape[0]
    assert depth == 2, "layer plan: mixer A/B layer (updates the context stream) then mixer C layer"

    cond = jnp.concatenate([c, c_ctx[None], jnp.zeros((8 - bsz - 1, d), F32)], axis=0)
    mod = _ada(cond, ada_w, ada_b)

    wg_all, wu_all, wd_all = moe_w_gate, moe_w_up, moe_w_down
    xl = x
    xc = ctx
    for i in range(depth):
        upd_ctx = i < depth - 1
        sh1, sc1, g1, sh2, sc2, g2 = [mod[i, :bsz, k * d:(k + 1) * d].reshape(bsz, 1, d) for k in range(6)]
        csh1, csc1, cg1, csh2, csc2, cg2 = [
            jnp.broadcast_to(mod[i, bsz, k * d:(k + 1) * d].reshape(1, 1, d), (bsz, 1, d)) for k in range(6)]
        gam1 = norm1_g[i].reshape(1, d)
        gam2 = norm2_g[i].reshape(1, d)
        wr_split = _router_weight(router_w[i])
        j = i // 2
        if i % 2 == 0:
            w_in = ab_w_in[j].astype(BF16)
            w_out = ab_w_out[j].astype(BF16)
            ones = jnp.ones((A_WIDTH,), F32)
            gain = jnp.concatenate([
                ones, a_vnorm_g[j].reshape(-1),
                jnp.tile(b_qnorm_g[j], B_HEADS) * (B_DIM ** -0.5),
                jnp.tile(b_knorm_g[j], B_HEADS), ones]).reshape(1, -1)
            p = _ab_in(xl, gam1, sh1, sc1, w_in, gain, tm=512)
            pc = _ab_in(xc, gam1, csh1, csc1, w_in, gain, tm=lc)
            ws = a_ws[j].astype(BF16)
            bs_b = jnp.broadcast_to(a_bs[j][:, :, None], (A_GROUPS, CHUNK, A_DIM)).astype(F32)
            b_l = _neighborhood_attention(p, pc, _na_bias_table(b_rpb[j], seq // GRID_W))
            xl, h2, aff_t = _ab_out(p, ws, bs_b, b_l, w_out, xl, g1, gam2, sh2, sc2, wr_split, tm=512)
            if upd_ctx:
                b_c = _context_attention(pc)
                xc, hc2, affc_t = _ab_out(pc, ws, bs_b, b_c, w_out, xc, cg1, gam2, csh2, csc2, wr_split, tm=lc)
        else:
            assert not upd_ctx
            bg, cz = _c_in(xl, gam1, sh1, sc1, c_w_in[j].astype(BF16), tm=512)
            xl, h2, aff_t = _c_out(bg, cz, c_conv_w[j], c_w_out[j].astype(BF16), xl, g1, gam2, sh2, sc2,
                                   wr_split, tm=512)

        xl = _ec_moe_residual(xl, h2, aff_t, g2, wg_all, wu_all, wd_all, i, tc=256)
        if upd_ctx:
            xc = _ec_moe_residual(xc, hc2, affc_t, cg2, wg_all, wu_all, wd_all, i, tc=CTX_MOE_TILE)
            xc = _slab_to_std(xc, d, tm=lc)
    return _slab_to_std(xl, d, tm=512)
```

```python
import functools
import math

import jax
import jax.numpy as jnp
import numpy as np
from jax import lax
from jax.experimental import pallas as pl
from jax.experimental.pallas import tpu as pltpu

GRID_W = 64
CHUNK = 128
A_GROUPS = 8
A_DIM = 128
A_WIDTH = A_GROUPS * A_DIM
B_HEADS = 8
B_DIM = 128
B_WIDTH = B_HEADS * B_DIM
NA_ROWS = 8
NA_COLS = 16
CONV_W = 3
N_EXPERTS = 16
CAP_FACTOR = 2
EPS = 1e-6

LANES = 128
MXU_COLS = 256
VMEM_LIMIT = 56 * 1024 * 1024

F32 = jnp.float32
BF16 = jnp.bfloat16
HIGHEST = lax.Precision.HIGHEST
MASK_VALUE = -1e30


SLAB_ROWS = 16
PITCH = 20


def _params(sem, **kw):
    return pltpu.CompilerParams(dimension_semantics=sem, vmem_limit_bytes=VMEM_LIMIT, **kw)


def _slab_cols(ref, lead, c, tm):
    return (*lead, pl.ds(c, tm, stride=PITCH), slice(None))


def _slab_zero_pad(ref, lead, tm):
    for c in range(SLAB_ROWS, PITCH):
        ref[_slab_cols(ref, lead, c, tm)] = jnp.zeros((tm, LANES), ref.dtype)


def _slab_load_rows(ref, lead, tm, dst_ref):
    for c in range(SLAB_ROWS):
        dst_ref[:, c * LANES:(c + 1) * LANES] = ref[_slab_cols(ref, lead, c, tm)]


def _silu(x):
    return x * (1.0 / (1.0 + jnp.exp(-x)))


def _gelu_tanh(x):
    return 0.5 * x * (1.0 + jnp.tanh(math.sqrt(2.0 / math.pi) * (x + 0.044715 * (x * x * x))))


def _norm_mod(x, gamma, shift, scale):
    ms = jnp.mean(x * x, axis=-1, keepdims=True)
    return (x * lax.rsqrt(ms + EPS) * gamma) * (1.0 + scale) + shift


def _ada_kernel(c_ref, w_ref, b_ref, o_ref):
    s = _silu(c_ref[...])
    o_ref[0] = jnp.dot(s, w_ref[0], precision=HIGHEST, preferred_element_type=F32) + b_ref[0]


def _ada(cond, ada_w, ada_b):
    depth, d, n6 = ada_w.shape
    tn = 1024
    return pl.pallas_call(
        _ada_kernel,
        grid=(depth, n6 // tn),
        in_specs=[
            pl.BlockSpec((8, d), lambda l, j: (0, 0)),
            pl.BlockSpec((1, d, tn), lambda l, j: (l, 0, j)),
            pl.BlockSpec((1, 1, tn), lambda l, j: (l, 0, j)),
        ],
        out_specs=pl.BlockSpec((1, 8, tn), lambda l, j: (l, 0, j)),
        out_shape=jax.ShapeDtypeStruct((depth, 8, n6), F32),
        compiler_params=_params(("arbitrary", "arbitrary")),
        name="ada_mod",
    )(cond, ada_w, ada_b.reshape(depth, 1, n6))


AB_SEGMENT_EPILOGUES = ("gelu", "gelu_norm", "norm", "norm", "none")


def _ab_in_kernel(x_ref, gam_ref, sh_ref, sc_ref, w_ref, gain_ref, o_ref, h_ref, *, tn):
    h_ref[...] = _norm_mod(x_ref[0], gam_ref[...], sh_ref[0], sc_ref[0]).astype(BF16)
    per_seg = A_WIDTH // tn
    for jt in range(w_ref.shape[1] // tn):
        kind = AB_SEGMENT_EPILOGUES[jt // per_seg]
        acc = jnp.dot(h_ref[...], w_ref[:, jt * tn:(jt + 1) * tn], preferred_element_type=F32)
        if kind.startswith("gelu"):
            acc = _gelu_tanh(acc)
        if kind.endswith("norm"):
            for g in range(tn // LANES):
                sl = slice(jt * tn + g * LANES, jt * tn + (g + 1) * LANES)
                ag = acc[:, g * LANES:(g + 1) * LANES]
                ms = jnp.mean(ag * ag, axis=-1, keepdims=True)
                o_ref[0, :, sl] = (ag * lax.rsqrt(ms + EPS) * gain_ref[:, sl]).astype(o_ref.dtype)
        else:
            o_ref[0, :, jt * tn:(jt + 1) * tn] = acc.astype(o_ref.dtype)


def _resident(shape):
    nd = len(shape)
    return pl.BlockSpec(shape, lambda *_: (0,) * nd, pipeline_mode=pl.Buffered(1))


def _ab_in(x, gamma, shift, scale, w, gain, tm, tn=512):
    b, n, d = x.shape
    f = w.shape[1]
    return pl.pallas_call(
        functools.partial(_ab_in_kernel, tn=tn),
        grid=(b, n // tm),
        in_specs=[
            pl.BlockSpec((1, tm, d), lambda bi, i: (bi, i, 0)),
            _resident((1, d)),
            pl.BlockSpec((1, 1, d), lambda bi, i: (bi, 0, 0)),
            pl.BlockSpec((1, 1, d), lambda bi, i: (bi, 0, 0)),
            _resident((d, f)),
            _resident((1, f)),
        ],
        out_specs=pl.BlockSpec((1, tm, f), lambda bi, i: (bi, i, 0)),
        out_shape=jax.ShapeDtypeStruct((b, n, f), BF16),
        scratch_shapes=[pltpu.VMEM((tm, d), BF16)],
        compiler_params=_params(("arbitrary", "arbitrary")),
        name="ab_in_proj",
    )(x, gamma, shift, scale, w, gain)


def _gmlp_tile(u_ref, v_ref, ws_ref, bs_ref, a_ref):
    for ch in range(a_ref.shape[0] // CHUNK):
        rows = slice(ch * CHUNK, (ch + 1) * CHUNK)
        for g in range(A_GROUPS):
            cols = slice(g * A_DIM, (g + 1) * A_DIM)
            s = jnp.dot(ws_ref[g], v_ref[0, rows, cols], preferred_element_type=F32) + bs_ref[g]
            a_ref[rows, cols] = (u_ref[0, rows, cols].astype(F32) * s).astype(a_ref.dtype)


Q_ROWS = 4
KV_BLOCKS = 3
MASKED_PLANE = 2 * NA_ROWS - 1


def _na_bias_table(rpb, rows):
    h = rpb.shape[0]
    cols = np.arange(GRID_W)
    cstart = np.clip(cols - NA_COLS // 2, 0, GRID_W - NA_COLS)
    valid = (cols[None, :] >= cstart[:, None]) & (cols[None, :] < cstart[:, None] + NA_COLS)
    col_off = np.clip(cols[None, :] - cols[:, None] + (NA_COLS - 1), 0, 2 * NA_COLS - 2)
    planes = jnp.where(valid[None, None], rpb[:, :, col_off], MASK_VALUE)
    planes = jnp.concatenate([planes, jnp.full((h, 1, GRID_W, GRID_W), MASK_VALUE, planes.dtype)], axis=1)
    nblk = rows // Q_ROWS
    assert nblk >= KV_BLOCKS + 1 and NA_ROWS <= (KV_BLOCKS - 1) * Q_ROWS
    sel = np.full((3, Q_ROWS, KV_BLOCKS * Q_ROWS), MASKED_PLANE, np.int32)
    for variant, g in enumerate((0, 1, nblk - 1)):
        first = min(max(g - 1, 0), nblk - KV_BLOCKS)
        for i in range(Q_ROWS):
            r = g * Q_ROWS + i
            rs = min(max(r - NA_ROWS // 2, 0), rows - NA_ROWS)
            for kr in range(KV_BLOCKS * Q_ROWS):
                key_row = first * Q_ROWS + kr
                if rs <= key_row < rs + NA_ROWS:
                    sel[variant, i, kr] = key_row - r + (NA_ROWS - 1)
    t = planes[:, sel]
    t = jnp.transpose(t, (1, 0, 2, 4, 3, 5))
    return t.reshape(3, h, Q_ROWS * GRID_W, KV_BLOCKS * Q_ROWS * GRID_W).astype(F32)


def _na_kernel(q_ref, k0_ref, k1_ref, k2_ref, v0_ref, v1_ref, v2_ref, kx_ref, vx_ref, bias_ref, o_ref):
    dn = (((1,), (1,)), ((), ()))
    tk = k0_ref.shape[1]
    k_refs = (k0_ref, k1_ref, k2_ref)
    v_refs = (v0_ref, v1_ref, v2_ref)
    for h in range(B_HEADS):
        cols = slice(h * B_DIM, (h + 1) * B_DIM)
        q = q_ref[0, :, cols]
        s = [lax.dot_general(q, k_refs[n][0, :, cols], dn, preferred_element_type=F32)
             + bias_ref[0, h, :, n * tk:(n + 1) * tk] for n in range(KV_BLOCKS)]
        s.append(lax.dot_general(q, kx_ref[0, :, cols], dn, preferred_element_type=F32))
        m = functools.reduce(jnp.maximum, [jnp.max(x, axis=-1, keepdims=True) for x in s])
        p = [jnp.exp(x - m) for x in s]
        l = functools.reduce(jnp.add, [jnp.sum(x, axis=-1, keepdims=True) for x in p])
        o = jnp.dot(p[KV_BLOCKS].astype(BF16), vx_ref[0, :, cols], preferred_element_type=F32)
        for n in range(KV_BLOCKS):
            o += jnp.dot(p[n].astype(BF16), v_refs[n][0, :, cols], preferred_element_type=F32)
        o_ref[0, :, cols] = (o / l).astype(o_ref.dtype)


def _neighborhood_attention(p, pc, bias):
    b, s, _ = p.shape
    lc = pc.shape[1]
    tq = Q_ROWS * GRID_W
    nblk = s // tq
    qcol, kcol, vcol = 2, 3, 4
    first = lambda i: jnp.clip(i - 1, 0, nblk - KV_BLOCKS)
    kv = lambda col, n: pl.BlockSpec((1, tq, B_WIDTH), lambda bi, i: (bi, first(i) + n, col))
    variant = lambda i: jnp.where(i == 0, 0, jnp.where(i == nblk - 1, 2, 1))
    return pl.pallas_call(
        _na_kernel,
        grid=(b, nblk),
        in_specs=[
            pl.BlockSpec((1, tq, B_WIDTH), lambda bi, i: (bi, i, qcol)),
            *[kv(kcol, n) for n in range(KV_BLOCKS)],
            *[kv(vcol, n) for n in range(KV_BLOCKS)],
            pl.BlockSpec((1, lc, B_WIDTH), lambda bi, i: (bi, 0, kcol)),
            pl.BlockSpec((1, lc, B_WIDTH), lambda bi, i: (bi, 0, vcol)),
            pl.BlockSpec((1, *bias.shape[1:]), lambda bi, i: (variant(i), 0, 0, 0)),
        ],
        out_specs=pl.BlockSpec((1, tq, B_WIDTH), lambda bi, i: (bi, i, 0)),
        out_shape=jax.ShapeDtypeStruct((b, s, B_WIDTH), BF16),
        compiler_params=_params(("arbitrary", "arbitrary")),
        name="neighborhood_attention",
    )(p, *([p] * (2 * KV_BLOCKS)), pc, pc, bias)


def _ctx_attn_kernel(q_ref, k_ref, v_ref, o_ref):
    for h in range(B_HEADS):
        cols = slice(h * B_DIM, (h + 1) * B_DIM)
        s = lax.dot_general(q_ref[0, :, cols], k_ref[0, :, cols], (((1,), (1,)), ((), ())),
                            preferred_element_type=F32)
        p = jnp.exp(s - jnp.max(s, axis=-1, keepdims=True))
        o = jnp.dot(p.astype(BF16), v_ref[0, :, cols], preferred_element_type=F32)
        o_ref[0, :, cols] = (o / jnp.sum(p, axis=-1, keepdims=True)).astype(o_ref.dtype)


def _context_attention(pc):
    b, lc, _ = pc.shape
    spec = lambda col: pl.BlockSpec((1, lc, B_WIDTH), lambda bi: (bi, 0, col))
    return pl.pallas_call(
        _ctx_attn_kernel,
        grid=(b,),
        in_specs=[spec(2), spec(3), spec(4)],
        out_specs=pl.BlockSpec((1, lc, B_WIDTH), lambda bi: (bi, 0, 0)),
        out_shape=jax.ShapeDtypeStruct((b, lc, B_WIDTH), BF16),
        compiler_params=_params(("arbitrary",)),
        name="context_attention",
    )(pc, pc, pc)


def _store_residual_and_route(xf_ref, gam_ref, sh_ref, sc_ref, wr_ref, o_ref, h_ref, aff_ref):
    tm = xf_ref.shape[0]
    xn = xf_ref[...]
    h = _norm_mod(xn, gam_ref[...], sh_ref[0], sc_ref[0])
    _slab_zero_pad(o_ref, (0,), tm)
    _slab_zero_pad(h_ref, (0,), tm)
    for c in range(SLAB_ROWS):
        sl = slice(c * LANES, (c + 1) * LANES)
        o_ref[_slab_cols(o_ref, (0,), c, tm)] = xn[:, sl]
        h_ref[_slab_cols(h_ref, (0,), c, tm)] = h[:, sl]
    ne = aff_ref.shape[1]
    h_hi = h.astype(BF16)
    h_lo = (h - h_hi.astype(F32)).astype(BF16)
    p_hi = jnp.dot(h_hi, wr_ref[...], preferred_element_type=F32).T
    p_lo = jnp.dot(h_lo, wr_ref[...], preferred_element_type=F32).T
    logits = p_hi[0:ne] + (p_hi[ne:2 * ne] + p_lo[0:ne])
    m = jnp.max(logits, axis=0, keepdims=True)
    e = jnp.exp(logits - m)
    aff_ref[0] = e / jnp.sum(e, axis=0, keepdims=True)


def _router_weight(router_w):
    d, ne = router_w.shape
    w_hi = router_w.astype(BF16)
    w_lo = (router_w - w_hi.astype(F32)).astype(BF16)
    return jnp.concatenate([w_hi, w_lo, jnp.zeros((d, LANES - 2 * ne), BF16)], axis=1)


def _route_specs(b, n, d, ne, tm):
    slab = pl.BlockSpec((1, tm * PITCH, LANES), lambda bi, i: (bi, i, 0))
    mod = pl.BlockSpec((1, 1, d), lambda bi, i: (bi, 0, 0))
    in_specs = [_resident((1, d)), mod, mod, _resident((d, LANES))]
    out_specs = [slab, slab, pl.BlockSpec((1, ne, tm), lambda bi, i: (bi, 0, i))]
    slab_shape = jax.ShapeDtypeStruct((b, n * PITCH, LANES), F32)
    return in_specs, out_specs, [slab_shape, slab_shape, jax.ShapeDtypeStruct((b, ne, n), F32)]


def _ab_out_kernel(u_ref, v_ref, ws_ref, bs_ref, b_ref, w_ref, x_ref, g_ref, gam_ref, sh_ref, sc_ref, wr_ref,
                   o_ref, h_ref, aff_ref, xf_ref, a_ref, *, tn):
    _gmlp_tile(u_ref, v_ref, ws_ref, bs_ref, a_ref)
    ka = a_ref.shape[1]
    for jt in range(w_ref.shape[1] // tn):
        cols = slice(jt * tn, (jt + 1) * tn)
        acc = jnp.dot(a_ref[...], w_ref[0:ka, cols], preferred_element_type=F32)
        acc += jnp.dot(b_ref[0], w_ref[ka:2 * ka, cols], preferred_element_type=F32)
        xf_ref[:, cols] = x_ref[0, :, cols] + g_ref[0, :, cols] * acc
    _store_residual_and_route(xf_ref, gam_ref, sh_ref, sc_ref, wr_ref, o_ref, h_ref, aff_ref)


def _ab_out(p, ws, bs_b, bm, w, x, gate, gamma2, shift2, scale2, wr_split, tm, tn=512):
    b, n, d = x.shape
    rin, rout, rshape = _route_specs(b, n, d, N_EXPERTS, tm)
    return pl.pallas_call(
        functools.partial(_ab_out_kernel, tn=tn),
        grid=(b, n // tm),
        in_specs=[
            pl.BlockSpec((1, tm, A_WIDTH), lambda bi, i: (bi, i, 0)),
            pl.BlockSpec((1, tm, A_WIDTH), lambda bi, i: (bi, i, 1)),
            _resident(ws.shape),
            _resident(bs_b.shape),
            pl.BlockSpec((1, tm, B_WIDTH), lambda bi, i: (bi, i, 0)),
            _resident(w.shape),
            pl.BlockSpec((1, tm, d), lambda bi, i: (bi, i, 0)),
            pl.BlockSpec((1, 1, d), lambda bi, i: (bi, 0, 0)),
            *rin,
        ],
        out_specs=rout,
        out_shape=rshape,
        scratch_shapes=[pltpu.VMEM((tm, d), F32), pltpu.VMEM((tm, A_WIDTH), BF16)],
        compiler_params=_params(("arbitrary", "arbitrary")),
        name="ab_out_proj",
    )(p, p, ws, bs_b, bm, w, x, gate, gamma2, shift2, scale2, wr_split)


def _c_in_kernel(x_ref, gam_ref, sh_ref, sc_ref, w_ref, bg_ref, cz_ref, h_ref, xf_ref, *, tn):
    _slab_load_rows(x_ref, (0,), h_ref.shape[0], xf_ref)
    h_ref[...] = _norm_mod(xf_ref[...], gam_ref[...], sh_ref[0], sc_ref[0]).astype(BF16)
    cw = w_ref.shape[1] // 3
    for jt in range(cw // tn):
        cols = slice(jt * tn, (jt + 1) * tn)
        proj = lambda seg: jnp.dot(h_ref[...], w_ref[:, seg * cw + jt * tn:seg * cw + (jt + 1) * tn],
                                   preferred_element_type=F32)
        bg_ref[0, :, cols] = proj(0).astype(bg_ref.dtype)
        cz_ref[0, :, cols] = (proj(1) * proj(2)).astype(cz_ref.dtype)


def _c_in(x, gamma, shift, scale, w, tm, tn=512):
    b = x.shape[0]
    n = x.shape[1] // PITCH
    d = w.shape[0]
    cw = w.shape[1] // 3
    out = jax.ShapeDtypeStruct((b, n, cw), BF16)
    ospec = pl.BlockSpec((1, tm, cw), lambda bi, i: (bi, i, 0))
    return pl.pallas_call(
        functools.partial(_c_in_kernel, tn=tn),
        grid=(b, n // tm),
        in_specs=[
            pl.BlockSpec((1, tm * PITCH, LANES), lambda bi, i: (bi, i, 0)),
            _resident((1, d)),
            pl.BlockSpec((1, 1, d), lambda bi, i: (bi, 0, 0)),
            pl.BlockSpec((1, 1, d), lambda bi, i: (bi, 0, 0)),
            _resident(w.shape),
        ],
        out_specs=[ospec, ospec],
        out_shape=[out, out],
        scratch_shapes=[pltpu.VMEM((tm, d), BF16), pltpu.VMEM((tm, d), F32)],
        compiler_params=_params(("arbitrary", "arbitrary")),
        name="c_in_proj",
    )(x, gamma, shift, scale, w)


HALO = 8


def _c_out_kernel(bg_ref, cz_ref, czp_ref, czn_ref, cw_ref, w_ref, x_ref, g_ref, gam_ref, sh_ref, sc_ref, wr_ref,
                  o_ref, h_ref, aff_ref, xf_ref, acc_ref, *, kchunk, tn):
    i = pl.program_id(1)
    ni = pl.num_programs(1)
    tm = cz_ref.shape[1]
    cw = cz_ref.shape[2]
    _slab_load_rows(x_ref, (0,), tm, xf_ref)
    row = lax.broadcasted_iota(jnp.int32, (tm, kchunk), 0)
    for kc in range(cw // kchunk):
        ks = slice(kc * kchunk, (kc + 1) * kchunk)
        cz = cz_ref[0, :, ks].astype(F32)
        prev_row = jnp.where(i > 0, czp_ref[0, HALO - 1:HALO, ks].astype(F32), 0.0)
        next_row = jnp.where(i < ni - 1, czn_ref[0, 0:1, ks].astype(F32), 0.0)
        up = jnp.where(row == 0, prev_row, pltpu.roll(cz, 1, 0))
        dn = jnp.where(row == tm - 1, next_row, pltpu.roll(cz, tm - 1, 0))
        y = cw_ref[0:1, ks] * up + cw_ref[1:2, ks] * cz + cw_ref[2:3, ks] * dn
        lhs = (bg_ref[0, :, ks].astype(F32) * y).astype(BF16)
        for jt in range(w_ref.shape[1] // tn):
            cols = slice(jt * tn, (jt + 1) * tn)
            part = jnp.dot(lhs, w_ref[ks, cols], preferred_element_type=F32)
            if kc == 0:
                acc_ref[:, cols] = part
            else:
                acc_ref[:, cols] += part
    xf_ref[...] = xf_ref[...] + g_ref[0] * acc_ref[...]
    _store_residual_and_route(xf_ref, gam_ref, sh_ref, sc_ref, wr_ref, o_ref, h_ref, aff_ref)


def _c_out(bg, cz, conv_w, w, x, gate, gamma2, shift2, scale2, wr_split, tm, tn=512):
    b, n, cw = bg.shape
    d = w.shape[1]
    hb = tm // HALO
    nh = n // HALO
    rin, rout, rshape = _route_specs(b, n, d, N_EXPERTS, tm)
    return pl.pallas_call(
        functools.partial(_c_out_kernel, kchunk=512, tn=tn),
        grid=(b, n // tm),
        in_specs=[
            pl.BlockSpec((1, tm, cw), lambda bi, i: (bi, i, 0)),
            pl.BlockSpec((1, tm, cw), lambda bi, i: (bi, i, 0)),
            pl.BlockSpec((1, HALO, cw), lambda bi, i: (bi, jnp.maximum(i * hb - 1, 0), 0)),
            pl.BlockSpec((1, HALO, cw), lambda bi, i: (bi, jnp.minimum((i + 1) * hb, nh - 1), 0)),
            _resident((CONV_W, cw)),
            _resident(w.shape),
            pl.BlockSpec((1, tm * PITCH, LANES), lambda bi, i: (bi, i, 0)),
            pl.BlockSpec((1, 1, d), lambda bi, i: (bi, 0, 0)),
            *rin,
        ],
        out_specs=rout,
        out_shape=rshape,
        scratch_shapes=[pltpu.VMEM((tm, d), F32), pltpu.VMEM((tm, d), F32)],
        compiler_params=_params(("arbitrary", "arbitrary")),
        name="c_out_proj",
    )(bg, cz, cz, cz, conv_w, w, x, gate, gamma2, shift2, scale2, wr_split)


SEARCH_BITS = 3
SELECT_GROUP = 8


def _select_kernel(aff_ref, idx_ref, gate_ref, *, cap, cchunk):
    group = aff_ref.shape[1]
    vs = [aff_ref[0, g] for g in range(group)]

    def count(mask):
        return jnp.sum(jnp.sum(mask.astype(F32), axis=1, keepdims=True), axis=0, keepdims=True)

    def search(it, ts):
        shift = 30 - SEARCH_BITS * (it + 1)
        out = []
        for v, t in zip(vs, ts):
            digit = jnp.zeros((1, 1), jnp.int32)
            for j in range(1, 2 ** SEARCH_BITS):
                cand = pltpu.bitcast(t | (jnp.int32(j) << shift), F32)
                digit += (count(v >= cand) >= cap).astype(jnp.int32)
            out.append(t | (digit << shift))
        return tuple(out)

    thrs = lax.fori_loop(0, 30 // SEARCH_BITS, search, tuple(jnp.zeros((1, 1), jnp.int32) for _ in vs))
    for g, (v, thr_bits) in enumerate(zip(vs, thrs)):
        _compact_selected(v, pltpu.bitcast(thr_bits, F32), idx_ref.at[0, g], gate_ref.at[0, g], cap, cchunk, count)


def _compact_selected(v, thr, idx_ref, gate_ref, cap, cchunk, count):
    nr = v.shape[0]
    gt = v > thr
    eq = v == thr
    need = cap - count(gt)

    lane_l = lax.broadcasted_iota(jnp.int32, (LANES, LANES), 0)
    lane_c = lax.broadcasted_iota(jnp.int32, (LANES, LANES), 1)
    tri_lane = (lane_l <= lane_c).astype(BF16)
    row_r = lax.broadcasted_iota(jnp.int32, (nr, nr), 0)
    row_c = lax.broadcasted_iota(jnp.int32, (nr, nr), 1)
    tri_row = (row_c <= row_r).astype(BF16)

    def prefix(mask):
        mf = mask.astype(BF16)
        in_row = jnp.dot(mf, tri_lane, preferred_element_type=F32)
        colcum = jnp.dot(tri_row, mf, preferred_element_type=F32)
        row_incl = jnp.sum(colcum, axis=1, keepdims=True)
        row_tot = jnp.sum(mask.astype(F32), axis=1, keepdims=True)
        return in_row, row_incl - row_tot, row_incl

    eq_in, eq_off, _ = prefix(eq)
    eq_rank = eq_in + eq_off - eq.astype(F32)
    sel = gt | (eq & (eq_rank < need))
    _, sel_off, sel_incl = prefix(sel)

    self_bf = sel.astype(BF16)
    tri_lane_t = (lane_c <= lane_l).astype(BF16)
    dn_t = (((1,), (1,)), ((), ()))
    pt = lax.dot_general(tri_lane_t, self_bf, dn_t, preferred_element_type=F32)
    eye = (lane_l == lane_c).astype(F32)
    vt = lax.dot_general(eye, v, dn_t, precision=HIGHEST, preferred_element_type=F32)

    for c0 in range(0, cap, cchunk):
        cc = min(cchunk, cap - c0)
        slot = (lax.broadcasted_iota(jnp.int32, (1, cc), 1) + c0).astype(F32)
        r_of = jnp.sum((sel_incl <= slot).astype(F32), axis=0, keepdims=True)
        onehot = (lax.broadcasted_iota(jnp.int32, (nr, cc), 0).astype(F32) == r_of)
        onehot_f = onehot.astype(F32)
        local = slot - jnp.sum(onehot_f * sel_off, axis=0, keepdims=True)
        prow = jnp.dot(pt.astype(BF16), onehot.astype(BF16), preferred_element_type=F32)
        l_of = jnp.sum((prow <= local).astype(F32), axis=0, keepdims=True)
        vrow = jnp.dot(vt, onehot_f, precision=HIGHEST, preferred_element_type=F32)
        lane_i = lax.broadcasted_iota(jnp.int32, (LANES, cc), 0).astype(F32)
        gsel = jnp.sum(jnp.where(lane_i == l_of, vrow, 0.0), axis=0, keepdims=True)
        idx_ref[:, c0:c0 + cc] = (r_of * LANES + l_of).astype(jnp.int32)
        gate_ref[:, c0:c0 + cc] = gsel


MIN_SELECT_ROWS = 8


def _select(aff_t, cap):
    b, ne, n = aff_t.shape
    if n < MIN_SELECT_ROWS * LANES:
        assert cap <= n
        aff_t = jnp.pad(aff_t, ((0, 0), (0, 0), (0, MIN_SELECT_ROWS * LANES - n)), constant_values=-1.0)
        n = MIN_SELECT_ROWS * LANES
    nr = n // LANES
    group = SELECT_GROUP
    assert ne % group == 0
    out = lambda dt: jax.ShapeDtypeStruct((b, ne, 1, cap), dt)
    ospec = pl.BlockSpec((1, group, 1, cap), lambda bi, e: (bi, e, 0, 0))
    idx, gate = pl.pallas_call(
        functools.partial(_select_kernel, cap=cap, cchunk=512),
        grid=(b, ne // group),
        in_specs=[pl.BlockSpec((1, group, nr, LANES), lambda bi, e: (bi, e, 0, 0))],
        out_specs=[ospec, ospec],
        out_shape=[out(jnp.int32), out(F32)],
        compiler_params=_params(("arbitrary", "arbitrary")),
        name="moe_select",
    )(aff_t.reshape(b, ne, nr, LANES))
    return idx.reshape(b, ne, cap), gate.reshape(b, ne, cap)


X_SLOTS = 2
O_SLOTS = 3
GATHER_UNROLL = 16
SCATTER_UNROLL = 8


def _moe_kernel(idx_ref, idxn_ref, g_ref, g2_ref, h_hbm, wg_hbm, wu_hbm, wd_hbm, x_hbm, o_hbm,
                xg, og, xs_ref, hid_ref, wg_buf, wu_buf, wd_buf, wg_stage, wu_stage, wd_stage, sem_x, sem_o, sem_s, sem_w,
                *, layer, nb, nt):
    del x_hbm
    e, bi, t = pl.program_id(0), pl.program_id(1), pl.program_id(2)
    ne = pl.num_programs(0)
    step = (e * nb + bi) * nt + t
    last = ne * nb * nt - 1
    tc = xs_ref.shape[0]
    moved = tc * SLAB_ROWS

    chunks = nb * nt
    chunk = bi * nt + t
    wslot = e % 2
    stages = (wg_stage, wu_stage, wd_stage)
    bufs = (wg_buf, wu_buf, wd_buf)

    def weight_rows(ref, k):
        rows = ref.shape[0]
        return pl.ds(pl.multiple_of(k * rows, rows), rows)

    def weight_copies(expert, k):
        return [pltpu.make_async_copy(hbm.at[layer, expert, weight_rows(stage, k), :], stage, sem_w.at[i])
                for i, (hbm, stage) in enumerate(zip((wg_hbm, wu_hbm, wd_hbm), stages))]

    def cast_chunk(slot, k):
        for stage, buf in zip(stages, bufs):
            buf[slot, weight_rows(stage, k), :] = stage[...].astype(BF16)

    @pl.when(step == 0)
    def _():
        def load(k, carry):
            copies = weight_copies(0, k)
            for cp in copies:
                cp.start()
            for cp in copies:
                cp.wait()
            cast_chunk(0, k)
            return carry
        lax.fori_loop(0, chunks, load, 0)

    @pl.when(e + 1 < ne)
    def _():
        for cp in weight_copies(e + 1, chunk):
            cp.start()

    def for_each_slot_row(row_ref, fn, unroll):
        def body(it, carry):
            for u in range(unroll):
                s = it * unroll + u
                fn(pl.ds(pl.multiple_of(row_ref[0, 0, s], 4), SLAB_ROWS),
                   pl.ds(pl.multiple_of(s * PITCH, 4), SLAB_ROWS), u)
            return carry
        lax.fori_loop(0, tc // unroll, body, 0)

    def issue_gathers(row_ref, xslot, oslot):
        def one(src, dst, u):
            pltpu.make_async_copy(h_hbm.at[src, :], xg.at[xslot, dst, :], sem_x.at[xslot]).start(priority=0)
            pltpu.make_async_copy(o_hbm.at[src, :], og.at[oslot, dst, :], sem_o.at[oslot]).start(priority=1)
        for_each_slot_row(row_ref, one, GATHER_UNROLL)

    def wait_rows(hbm, buf, slot, sem, to_hbm):
        a, b = hbm.at[pl.ds(0, moved), :], buf.at[slot, pl.ds(0, moved), :]
        (pltpu.make_async_copy(b, a, sem.at[slot]) if to_hbm else pltpu.make_async_copy(a, b, sem.at[slot])).wait()

    @pl.when(step == 0)
    def _():
        issue_gathers(idx_ref, 0, 0)

    @pl.when(step >= 2)
    def _():
        wait_rows(o_hbm, og, (step - 2) % O_SLOTS, sem_s, True)

    @pl.when(step < last)
    def _():
        issue_gathers(idxn_ref, (step + 1) % X_SLOTS, (step + 1) % O_SLOTS)

    xslot = step % X_SLOTS
    oslot = step % O_SLOTS
    wait_rows(h_hbm, xg, xslot, sem_x, False)
    for c in range(SLAB_ROWS):
        xs_ref[:, c * LANES:(c + 1) * LANES] = xg[xslot, pl.ds(c, tc, stride=PITCH), :].astype(BF16)
    xs = xs_ref[...]
    eye = lax.broadcasted_iota(jnp.int32, (tc, tc), 0) == lax.broadcasted_iota(jnp.int32, (tc, tc), 1)
    gcol = jnp.sum(jnp.where(eye, g_ref[0], 0.0), axis=1, keepdims=True)
    for fc in range(hid_ref.shape[1] // MXU_COLS):
        fs = slice(fc * MXU_COLS, (fc + 1) * MXU_COLS)
        gate = jnp.dot(xs, wg_buf[wslot, :, fs], preferred_element_type=F32)
        up = jnp.dot(xs, wu_buf[wslot, :, fs], preferred_element_type=F32)
        hid_ref[:, fs] = (_silu(gate) * up * gcol).astype(BF16)
    hid = hid_ref[...]

    wait_rows(o_hbm, og, oslot, sem_o, False)
    per = MXU_COLS // LANES
    for dc in range(SLAB_ROWS // per):
        y = jnp.dot(hid, wd_buf[wslot, :, dc * MXU_COLS:(dc + 1) * MXU_COLS], preferred_element_type=F32)
        for k in range(per):
            c = dc * per + k
            rows = (oslot, pl.ds(c, tc, stride=PITCH), slice(None))
            og[rows] = og[rows] + g2_ref[0][:, c * LANES:(c + 1) * LANES] * y[:, k * LANES:(k + 1) * LANES]

    def scatter(src, dst, u):
        pltpu.make_async_copy(og.at[oslot, dst, :], o_hbm.at[src, :], sem_s.at[oslot]).start(priority=u % 2)
    for_each_slot_row(idx_ref, scatter, SCATTER_UNROLL)

    @pl.when(e + 1 < ne)
    def _():
        for cp in weight_copies(e + 1, chunk):
            cp.wait()
        cast_chunk(1 - wslot, chunk)

    @pl.when(step == last)
    def _():
        @pl.when(step >= 1)
        def _():
            wait_rows(o_hbm, og, (step - 1) % O_SLOTS, sem_s, True)
        wait_rows(o_hbm, og, oslot, sem_s, True)


def _moe_experts(idx, gate, h, wg, wu, wd, layer, gate2, x, tc):
    b, ne, cap = idx.shape
    n = x.shape[1] // PITCH
    d, f = wg.shape[2], wg.shape[3]
    nt = cap // tc
    assert b >= 2 and nt >= 2 and tc % GATHER_UNROLL == 0 and tc % SCATTER_UNROLL == 0 and d == SLAB_ROWS * LANES
    nsteps = ne * b * nt

    def cur(e, bi, t):
        return ((bi * ne + e) * nt + t, 0, 0)

    def nxt(e, bi, t):
        step = jnp.minimum((e * b + bi) * nt + t + 1, nsteps - 1)
        return (((step // nt) % b * ne + step // (nt * b)) * nt + step % nt, 0, 0)

    first_row = (jnp.arange(b, dtype=jnp.int32)[:, None, None] * n + idx) * PITCH
    idx3 = first_row.reshape(b * ne * nt, 1, tc)
    rows = (tc * PITCH, LANES)
    chunks = b * nt
    assert d % (16 * chunks) == 0 and f % (16 * chunks) == 0
    hbm = pl.BlockSpec(memory_space=pl.ANY)
    out = pl.pallas_call(
        functools.partial(_moe_kernel, layer=layer, nb=b, nt=nt),
        grid=(ne, b, nt),
        in_specs=[
            pl.BlockSpec((1, 1, tc), cur, memory_space=pltpu.SMEM),
            pl.BlockSpec((1, 1, tc), nxt, memory_space=pltpu.SMEM),
            pl.BlockSpec((1, 1, tc), cur),
            pl.BlockSpec((1, 1, d), lambda e, bi, t: (bi, 0, 0)),
            hbm, hbm, hbm, hbm, hbm,
        ],
        out_specs=hbm,
        out_shape=jax.ShapeDtypeStruct((b * n * PITCH, LANES), F32),
        input_output_aliases={8: 0},
        scratch_shapes=[
            pltpu.VMEM((X_SLOTS, *rows), F32), pltpu.VMEM((O_SLOTS, *rows), F32), pltpu.VMEM((tc, d), BF16),
            pltpu.VMEM((tc, f), BF16),
            pltpu.VMEM((2, d, f), BF16), pltpu.VMEM((2, d, f), BF16), pltpu.VMEM((2, f, d), BF16),
            pltpu.VMEM((d // chunks, f), F32), pltpu.VMEM((d // chunks, f), F32), pltpu.VMEM((f // chunks, d), F32),
            pltpu.SemaphoreType.DMA((X_SLOTS,)), pltpu.SemaphoreType.DMA((O_SLOTS,)),
            pltpu.SemaphoreType.DMA((O_SLOTS,)), pltpu.SemaphoreType.DMA((3,)),
        ],
        compiler_params=_params(("arbitrary", "arbitrary", "arbitrary"), disable_bounds_checks=True),
        name="moe_experts",
    )(idx3, idx3, gate.reshape(b * ne * nt, 1, tc), gate2, h.reshape(b * n * PITCH, LANES), wg, wu, wd,
      x.reshape(b * n * PITCH, LANES))
    return out.reshape(x.shape)


def _ec_moe_residual(x, h, aff_t, gate2, wg, wu, wd, layer, *, tc):
    n = x.shape[1] // PITCH
    cap = CAP_FACTOR * n // N_EXPERTS
    idx, g = _select(aff_t, cap)
    return _moe_experts(idx, g, h, wg, wu, wd, layer, gate2, x, tc)


def _slab_to_std_kernel(x_ref, o_ref):
    _slab_load_rows(x_ref, (0,), o_ref.shape[1], o_ref.at[0])


def _slab_to_std(x, d, tm):
    b = x.shape[0]
    n = x.shape[1] // PITCH
    return pl.pallas_call(
        _slab_to_std_kernel,
        grid=(b, n // tm),
        in_specs=[pl.BlockSpec((1, tm * PITCH, LANES), lambda bi, i: (bi, i, 0))],
        out_specs=pl.BlockSpec((1, tm, d), lambda bi, i: (bi, i, 0)),
        out_shape=jax.ShapeDtypeStruct((b, n, d), F32),
        compiler_params=_params(("arbitrary", "arbitrary")),
        name="slab_to_std",
    )(x)


CTX_MOE_TILE = 16


def kernel(x, c, ctx, c_ctx, ada_w, ada_b, norm1_g, norm2_g, ab_w_in, ab_w_out, a_ws, a_bs, a_vnorm_g,
           b_qnorm_g, b_knorm_g, b_rpb, c_w_in, c_conv_w, c_w_out, router_w, moe_w_gate, moe_w_up, moe_w_down):
    bsz, seq, d = x.shape
    lc = ctx.shape[1]
    depth = ada_w.shape[0]
    assert depth == 2, "layer plan: mixer A/B layer (updates the context stream) then mixer C layer"

    cond = jnp.concatenate([c, c_ctx[None], jnp.zeros((8 - bsz - 1, d), F32)], axis=0)
    mod = _ada(cond, ada_w, ada_b)

    wg_all, wu_all, wd_all = moe_w_gate, moe_w_up, moe_w_down
    xl = x
    xc = ctx
    for i in range(depth):
        upd_ctx = i < depth - 1
        sh1, sc1, g1, sh2, sc2, g2 = [mod[i, :bsz, k * d:(k + 1) * d].reshape(bsz, 1, d) for k in range(6)]
        csh1, csc1, cg1, csh2, csc2, cg2 = [
            jnp.broadcast_to(mod[i, bsz, k * d:(k + 1) * d].reshape(1, 1, d), (bsz, 1, d)) for k in range(6)]
        gam1 = norm1_g[i].reshape(1, d)
        gam2 = norm2_g[i].reshape(1, d)
        wr_split = _router_weight(router_w[i])
        j = i // 2
        if i % 2 == 0:
            w_in = ab_w_in[j].astype(BF16)
            w_out = ab_w_out[j].astype(BF16)
            ones = jnp.ones((A_WIDTH,), F32)
            gain = jnp.concatenate([
                ones, a_vnorm_g[j].reshape(-1),
                jnp.tile(b_qnorm_g[j], B_HEADS) * (B_DIM ** -0.5),
                jnp.tile(b_knorm_g[j], B_HEADS), ones]).reshape(1, -1)
            p = _ab_in(xl, gam1, sh1, sc1, w_in, gain, tm=512)
            pc = _ab_in(xc, gam1, csh1, csc1, w_in, gain, tm=lc)
            ws = a_ws[j].astype(BF16)
            bs_b = jnp.broadcast_to(a_bs[j][:, :, None], (A_GROUPS, CHUNK, A_DIM)).astype(F32)
            b_l = _neighborhood_attention(p, pc, _na_bias_table(b_rpb[j], seq // GRID_W))
            xl, h2, aff_t = _ab_out(p, ws, bs_b, b_l, w_out, xl, g1, gam2, sh2, sc2, wr_split, tm=512)
            if upd_ctx:
                b_c = _context_attention(pc)
                xc, hc2, affc_t = _ab_out(pc, ws, bs_b, b_c, w_out, xc, cg1, gam2, csh2, csc2, wr_split, tm=lc)
        else:
            assert not upd_ctx
            bg, cz = _c_in(xl, gam1, sh1, sc1, c_w_in[j].astype(BF16), tm=512)
            xl, h2, aff_t = _c_out(bg, cz, c_conv_w[j], c_w_out[j].astype(BF16), xl, g1, gam2, sh2, sc2,
                                   wr_split, tm=512)

        xl = _ec_moe_residual(xl, h2, aff_t, g2, wg_all, wu_all, wd_all, i, tc=256)
        if upd_ctx:
            xc = _ec_moe_residual(xc, hc2, affc_t, cg2, wg_all, wu_all, wd_all, i, tc=CTX_MOE_TILE)
            xc = _slab_to_std(xc, d, tm=lc)
    return _slab_to_std(xl, d, tm=512)
```
